```python
import jax, jax.numpy as jnp
from jax import lax
import numpy as np

D_MODEL = 1024
BATCH = 16
SEQ = 2048
DEPTH = 4
DEC_BATCH = 8
DEC_SEQ = 64
PAST_LEN = 4096

CHUNK = 64
D_MIX = D_MODEL
ATTN_WIDTH = D_MIX // 2
RET_WIDTH = D_MIX - ATTN_WIDTH
N_HEADS = 8
HEAD_DIM = ATTN_WIDTH // N_HEADS
N_KV_HEADS = 2
KV_GROUP = N_HEADS // N_KV_HEADS
KV_WIDTH = N_KV_HEADS * HEAD_DIM
IDX_HEADS = 8
IDX_DIM = 64
TOPK_MAX = 256
Q_BLOCK = 128
RET_HEADS = 4
RET_DK = RET_WIDTH // RET_HEADS
RET_DV = RET_WIDTH // RET_HEADS
PLE_DIM = 256
RMS_EPS = 1e-6
NEG_INF = -1e30
ATTN_SCALE = HEAD_DIM ** -0.5
IDX_SCALE = IDX_DIM ** -0.5
IDX_HEAD_SCALE = IDX_HEADS ** -0.5
ROPE_BASE = 10000.0
SPLIT_SIZES = (ATTN_WIDTH, KV_WIDTH, KV_WIDTH, ATTN_WIDTH,
               IDX_HEADS * IDX_DIM, IDX_DIM, IDX_HEADS,
               RET_HEADS * RET_DK, RET_HEADS * RET_DK, RET_HEADS * RET_DV, RET_WIDTH)
D_IN = sum(SPLIT_SIZES)

kernel_name = "hybrid_dsa_retention_stream_step"


def rms_norm(x, g):
    xf = x.astype(jnp.float32)
    y = xf * lax.rsqrt(jnp.mean(xf * xf, axis=-1, keepdims=True) + RMS_EPS)
    return (y * g.astype(jnp.float32)).astype(x.dtype)


def split_cols(u):
    parts, start = [], 0
    for size in SPLIT_SIZES:
        parts.append(u[..., start:start + size])
        start += size
    return parts


def rotary(x, pos):
    half = x.shape[-1] // 2
    inv = ROPE_BASE ** (-jnp.arange(half, dtype=jnp.float32) / half)
    ang = pos.astype(jnp.float32)[:, None] * inv[None, :]
    cos = jnp.cos(ang)[None, :, None, :]
    sin = jnp.sin(ang)[None, :, None, :]
    x1, x2 = x[..., :half], x[..., half:]
    return jnp.concatenate([x1 * cos - x2 * sin, x1 * sin + x2 * cos], axis=-1)


def dsa_attention(q, q_idx, w_idx, k_all, v_all, k_idx_all, q_pos):
    B, T = q.shape[0], q.shape[1]
    L = k_all.shape[1]
    k_sel = min(TOPK_MAX, L // 4)
    qb = min(Q_BLOCK, T)
    nb = T // qb
    key_pos = jnp.arange(L, dtype=jnp.int32)

    def blockify(a):
        return jnp.swapaxes(a.reshape((B, nb, qb) + a.shape[2:]), 0, 1)

    def one_block(blk):
        qq, qi, wi, pos = blk
        chunk_end = (pos // CHUNK + 1) * CHUNK
        admissible = key_pos[None, :] < chunk_end[:, None]
        rel = jax.nn.relu(jnp.einsum('bqhd,bsd->bqhs', qi, k_idx_all).astype(jnp.float32) * IDX_SCALE)
        score = jnp.einsum('bqhs,bqh->bqs', rel, wi.astype(jnp.float32) * IDX_HEAD_SCALE)
        score = jnp.where(admissible[None], score, -jnp.inf)
        _, idx = lax.top_k(score, k_sel)
        sel_ok = idx < chunk_end[None, :, None]
        k_g = jax.vmap(lambda kb, ib: kb[ib])(k_all, idx)
        v_g = jax.vmap(lambda vb, ib: vb[ib])(v_all, idx)
        qg = qq.reshape(B, qb, N_KV_HEADS, KV_GROUP, HEAD_DIM)
        logits = jnp.einsum('bqgrd,bqkgd->bqgrk', qg, k_g).astype(jnp.float32) * ATTN_SCALE
        logits = jnp.where(sel_ok[:, :, None, None, :], logits, NEG_INF)
        probs = jax.nn.softmax(logits, axis=-1).astype(v_all.dtype)
        out = jnp.einsum('bqgrk,bqkgd->bqgrd', probs, v_g)
        return out.reshape(B, qb, ATTN_WIDTH)

    out = lax.map(one_block, (blockify(q), blockify(q_idx), blockify(w_idx), q_pos.reshape(nb, qb)))
    return jnp.swapaxes(out, 0, 1).reshape(B, T, ATTN_WIDTH)


def retention(q, k, v, r0, pos):
    B, T = q.shape[0], q.shape[1]
    q = rotary(q.astype(jnp.float32), pos)
    k = rotary(k.astype(jnp.float32), pos) * (RET_DK ** -0.5)
    v = v.astype(jnp.float32)
    c = min(CHUNK, T)
    nc = T // c
    log_g = jnp.log(1.0 - 2.0 ** (-5.0 - jnp.arange(RET_HEADS, dtype=jnp.float32)))
    i = jnp.arange(c, dtype=jnp.float32)
    diff = i[:, None] - i[None, :]
    decay_mask = jnp.where(diff[None] >= 0, jnp.exp(log_g[:, None, None] * jnp.maximum(diff, 0.0)[None]), 0.0)
    zeta = jnp.exp(log_g[:, None] * (c - 1 - i)[None, :])
    xi = jnp.exp(log_g[:, None] * (i + 1)[None, :])
    g_chunk = jnp.exp(log_g * c)
    qc = q.reshape(B, nc, c, RET_HEADS, RET_DK)
    kc = k.reshape(B, nc, c, RET_HEADS, RET_DK)
    vc = v.reshape(B, nc, c, RET_HEADS, RET_DV)
    inner = jnp.einsum('bnihd,bnjhd->bnhij', qc, kc) * decay_mask[None, None]
    o = jnp.einsum('bnhij,bnjhe->bnihe', inner, vc)
    u = jnp.einsum('bnjhd,bnjhe,hj->nbhde', kc, vc, zeta)

    def step(r, u_n):
        return g_chunk[None, :, None, None] * r + u_n, r

    r_final, r_prev = lax.scan(step, r0.astype(jnp.float32), u)
    o = o + jnp.einsum('bnihd,nbhde,hi->bnihe', qc, r_prev, xi)
    return o.reshape(B, T, RET_HEADS, RET_DV), r_final


def hybrid_layer(h, p, pos, past_k, past_v, past_k_idx, ret_state,
                 w_in, w_out, norm_g, q_norm_g, k_norm_g, ret_norm_g, ple_proj, ple_gate, ple_norm_g):
    B, T = h.shape[0], h.shape[1]
    u = rms_norm(h, norm_g) @ w_in
    (a_q, a_k, a_v, a_gate, i_q, i_k, i_w, r_q, r_k, r_v, r_gate) = split_cols(u)
    q = rms_norm(a_q.reshape(B, T, N_HEADS, HEAD_DIM), q_norm_g)
    k = rms_norm(a_k.reshape(B, T, N_KV_HEADS, HEAD_DIM), k_norm_g)
    v = a_v.reshape(B, T, N_KV_HEADS, HEAD_DIM)
    q_idx = i_q.reshape(B, T, IDX_HEADS, IDX_DIM)
    k_idx = i_k
    if past_k is None:
        k_all, v_all, k_idx_all = k, v, k_idx
    else:
        k_all = jnp.concatenate([past_k, k], axis=1)
        v_all = jnp.concatenate([past_v, v], axis=1)
        k_idx_all = jnp.concatenate([past_k_idx, k_idx], axis=1)
    attn = dsa_attention(q, q_idx, i_w, k_all, v_all, k_idx_all, pos)
    ret, r_new = retention(r_q.reshape(B, T, RET_HEADS, RET_DK), r_k.reshape(B, T, RET_HEADS, RET_DK),
                           r_v.reshape(B, T, RET_HEADS, RET_DV), ret_state, pos)
    ret = rms_norm(ret, ret_norm_g.reshape(RET_HEADS, RET_DV)).reshape(B, T, RET_WIDTH).astype(h.dtype)
    mixed = jnp.concatenate([attn * jax.nn.silu(a_gate), ret * jax.nn.silu(r_gate)], axis=-1)
    h = h + mixed @ w_out
    gate = jax.nn.sigmoid(rms_norm(h, ple_norm_g) @ ple_gate)
    h = h + gate * (p @ ple_proj)
    return h, k, v, k_idx, r_new


def setup_inputs(seed: int = 0) -> dict:
    key = jax.random.key(seed)
    ks = jax.random.split(key, 20)
    f32 = jnp.float32
    nrm = lambda k, shape, s: jax.random.normal(k, shape, f32) * s
    return {
        "x_prompt": nrm(ks[0], (BATCH, SEQ, D_MODEL), 1.0),
        "x_sample": nrm(ks[1], (DEC_BATCH, DEC_SEQ, D_MODEL), 1.0),
        "cache_k": nrm(ks[2], (DEPTH, DEC_BATCH, PAST_LEN, N_KV_HEADS, HEAD_DIM), 1.0),
        "cache_v": nrm(ks[3], (DEPTH, DEC_BATCH, PAST_LEN, N_KV_HEADS, HEAD_DIM), 1.0),
        "cache_k_idx": nrm(ks[4], (DEPTH, DEC_BATCH, PAST_LEN, IDX_DIM), 1.0),
        "state_ret": nrm(ks[5], (DEPTH, DEC_BATCH, RET_HEADS, RET_DK, RET_DV), 0.5),
        "p_prompt": nrm(ks[6], (DEPTH, BATCH, SEQ, PLE_DIM), 1.0),
        "p_sample": nrm(ks[7], (DEPTH, DEC_BATCH, DEC_SEQ, PLE_DIM), 1.0),
        "w_in": nrm(ks[8], (DEPTH, D_MODEL, D_IN), D_MODEL ** -0.5),
        "w_out": nrm(ks[9], (DEPTH, D_MIX, D_MODEL), D_MIX ** -0.5),
        "norm_g": 1.0 + nrm(ks[10], (DEPTH, D_MODEL), 0.05),
        "q_norm_g": 1.0 + nrm(ks[11], (DEPTH, HEAD_DIM), 0.05),
        "k_norm_g": 1.0 + nrm(ks[12], (DEPTH, HEAD_DIM), 0.05),
        "ret_norm_g": 1.0 + nrm(ks[13], (DEPTH, RET_WIDTH), 0.05),
        "ple_proj": nrm(ks[14], (DEPTH, PLE_DIM, D_MODEL), PLE_DIM ** -0.5),
        "ple_gate": nrm(ks[15], (DEPTH, D_MODEL, D_MODEL), D_MODEL ** -0.5),
        "ple_norm_g": 1.0 + nrm(ks[16], (DEPTH, D_MODEL), 0.05),
    }


def reference(x_prompt, x_sample, cache_k, cache_v, cache_k_idx, state_ret, p_prompt, p_sample,
              w_in, w_out, norm_g, q_norm_g, k_norm_g, ret_norm_g, ple_proj, ple_gate, ple_norm_g):
    b_p, t_p = x_prompt.shape[0], x_prompt.shape[1]
    t_s = x_sample.shape[1]
    past = cache_k.shape[2]
    pos_p = jnp.arange(t_p, dtype=jnp.int32)
    pos_s = past + jnp.arange(t_s, dtype=jnp.int32)
    r0 = jnp.zeros((b_p, RET_HEADS, RET_DK, RET_DV), jnp.float32)
    h_p, h_s = x_prompt, x_sample
    kp_l, vp_l, kip_l, rp_l, ks_l, vs_l, kis_l, rs_l = [], [], [], [], [], [], [], []
    for i in range(DEPTH):
        lw = (w_in[i], w_out[i], norm_g[i], q_norm_g[i], k_norm_g[i], ret_norm_g[i],
              ple_proj[i], ple_gate[i], ple_norm_g[i])
        h_p, kp, vp, kip, rp = hybrid_layer(h_p, p_prompt[i], pos_p, None, None, None, r0, *lw)
        h_s, ks, vs, kis, rs = hybrid_layer(h_s, p_sample[i], pos_s, cache_k[i], cache_v[i],
                                            cache_k_idx[i], state_ret[i], *lw)
        kp_l.append(kp); vp_l.append(vp); kip_l.append(kip); rp_l.append(rp)
        ks_l.append(ks); vs_l.append(vs); kis_l.append(kis); rs_l.append(rs)
    new_k_prompt = jnp.stack(kp_l)
    new_v_prompt = jnp.stack(vp_l)
    new_k_idx_prompt = jnp.stack(kip_l)
    new_state_ret_prompt = jnp.stack(rp_l)
    new_k_sample = jnp.stack(ks_l)
    new_v_sample = jnp.stack(vs_l)
    new_k_idx_sample = jnp.stack(kis_l)
    new_state_ret_sample = jnp.stack(rs_l)
    return (h_p, h_s, new_k_prompt, new_v_prompt, new_k_idx_prompt, new_state_ret_prompt,
            new_k_sample, new_v_sample, new_k_idx_sample, new_state_ret_sample)
```

```python
import functools

import jax
import jax.numpy as jnp
from jax import lax
from jax.experimental import pallas as pl
from jax.experimental.pallas import tpu as pltpu

F32 = jnp.float32
BF16 = jnp.bfloat16

D_MODEL = 1024
CHUNK = 64
ATTN_WIDTH = 512
RET_WIDTH = 512
N_HEADS = 8
HEAD_DIM = 64
N_KV_HEADS = 2
KV_GROUP = N_HEADS // N_KV_HEADS
KV_WIDTH = N_KV_HEADS * HEAD_DIM
IDX_HEADS = 8
IDX_DIM = 64
TOPK_MAX = 256
RET_HEADS = 4
RET_DK = 128
RET_DV = 128
PLE_DIM = 256
RMS_EPS = 1e-6
NEG_INF = -1e30
ATTN_SCALE = HEAD_DIM ** -0.5
IDX_SCALE = IDX_DIM ** -0.5
IDX_HEAD_SCALE = IDX_HEADS ** -0.5
ROPE_BASE = 10000.0

LANES = 128
INT_MIN = -(2 ** 31)
VMEM_LIMIT = 56 * 1024 * 1024

C_Q = 0
C_K = 512
C_V = 640
C_AG = 768
C_IQ = 1280
C_RQ = 1792
C_RK = 2304
C_RV = 2816
C_RG = 3328
C_IKW = 3840
W_COLS = 3968


def _dot(a, b):
    return jnp.dot(a, b, preferred_element_type=F32)


def _dot_nt(a, b):
    return lax.dot_general(a, b, (((1,), (1,)), ((), ())), preferred_element_type=F32)


def _proj_kernel(x_ref, ng_ref, w_ref, qg_ref, kg_ref, gq_ref, gk_ref, cos_ref, sin_ref,
                 qT_ref, kn_ref, v_ref, ag_ref, iqT_ref, ikw_ref, ikwT_ref,
                 rq_ref, rk_ref, rv_ref, rg_ref):
    x = x_ref[...]
    ms = jnp.mean(x * x, axis=-1, keepdims=True)
    xn = ((x * lax.rsqrt(ms + RMS_EPS)) * ng_ref[...]).astype(BF16)

    def proj(c0, c1):
        return _dot(xn, w_ref[:, c0:c1])

    def head_ms(u, gmat_ref):
        sq = u * u
        hi = sq.astype(BF16)
        lo = (sq - hi.astype(F32)).astype(BF16)
        return _dot(hi, gmat_ref[...]) + _dot(lo, gmat_ref[...])

    uq = proj(C_Q, C_K)
    qn = (uq * lax.rsqrt(head_ms(uq, gq_ref) + RMS_EPS)) * qg_ref[...]
    qT_ref[...] = (qn * ATTN_SCALE).T.astype(BF16)

    uk = proj(C_K, C_V)
    kn_ref[...] = (uk * lax.rsqrt(head_ms(uk, gk_ref) + RMS_EPS)) * kg_ref[...]
    v_ref[...] = proj(C_V, C_AG)
    ag_ref[...] = proj(C_AG, C_IQ)
    iqT_ref[...] = proj(C_IQ, C_RQ).T.astype(BF16)
    ikw = proj(C_IKW, W_COLS)
    ikw_ref[...] = ikw
    ikwT_ref[...] = ikw.T

    cosf = cos_ref[...]
    sinf = sin_ref[...]

    def rotary(u):
        parts = []
        for h in range(RET_HEADS):
            xh = u[:, h * RET_DK:(h + 1) * RET_DK]
            parts.append(xh * cosf + pltpu.roll(xh, RET_DK // 2, 1) * sinf)
        return jnp.concatenate(parts, axis=1)

    rq_ref[...] = rotary(proj(C_RQ, C_RK)).astype(BF16)
    rk_ref[...] = (rotary(proj(C_RK, C_RV)) * (RET_DK ** -0.5)).astype(BF16)
    rv_ref[...] = proj(C_RV, C_RG).astype(BF16)
    rg_ref[...] = proj(C_RG, C_IKW)


def _proj_call(x, ng, w, qg, kg, gq, gk, cosf, sinf, tm):
    B, T, _ = x.shape
    nt = T // tm
    row = lambda b, i: (b, i, 0)
    col = lambda b, i: (b, 0, i)
    const = lambda b, i: (0, 0)
    tok = lambda width: pl.BlockSpec((None, tm, width), row)
    tokT = lambda width: pl.BlockSpec((None, width, tm), col)
    out_shape = (
        jax.ShapeDtypeStruct((B, ATTN_WIDTH, T), BF16),
        jax.ShapeDtypeStruct((B, T, KV_WIDTH), F32),
        jax.ShapeDtypeStruct((B, T, KV_WIDTH), F32),
        jax.ShapeDtypeStruct((B, T, ATTN_WIDTH), F32),
        jax.ShapeDtypeStruct((B, ATTN_WIDTH, T), BF16),
        jax.ShapeDtypeStruct((B, T, LANES), F32),
        jax.ShapeDtypeStruct((B, LANES, T), F32),
        jax.ShapeDtypeStruct((B, T, RET_WIDTH), BF16),
        jax.ShapeDtypeStruct((B, T, RET_WIDTH), BF16),
        jax.ShapeDtypeStruct((B, T, RET_WIDTH), BF16),
        jax.ShapeDtypeStruct((B, T, RET_WIDTH), F32),
    )
    out_specs = (tokT(ATTN_WIDTH), tok(KV_WIDTH), tok(KV_WIDTH), tok(ATTN_WIDTH), tokT(ATTN_WIDTH),
                 tok(LANES), tokT(LANES), tok(RET_WIDTH), tok(RET_WIDTH), tok(RET_WIDTH), tok(RET_WIDTH))
    in_specs = [
        tok(D_MODEL),
        pl.BlockSpec((1, D_MODEL), const),
        pl.BlockSpec((D_MODEL, W_COLS), const),
        pl.BlockSpec((1, ATTN_WIDTH), const),
        pl.BlockSpec((1, KV_WIDTH), const),
        pl.BlockSpec((ATTN_WIDTH, ATTN_WIDTH), const),
        pl.BlockSpec((KV_WIDTH, KV_WIDTH), const),
        pl.BlockSpec((tm, LANES), lambda b, i: (i, 0)),
        pl.BlockSpec((tm, LANES), lambda b, i: (i, 0)),
    ]
    return pl.pallas_call(
        _proj_kernel, grid=(B, nt), in_specs=in_specs, out_specs=out_specs, out_shape=out_shape,
        compiler_params=pltpu.CompilerParams(dimension_semantics=("parallel", "parallel"),
                                             vmem_limit_bytes=VMEM_LIMIT),
        name="proj",
    )(x, ng, w, qg, kg, gq, gk, cosf, sinf)


def _attn_kernel(iqT_ref, qT_ref, ikwT_ref, ikw_ref, kn_ref, v_ref, tri_ref, out_ref,
                 sc_ref, acc_ref, *, pos0, ksel, nkb_max):
    j = pl.program_id(1)
    q0 = pos0 + j * LANES
    nkb = jnp.minimum(((((q0 + LANES - 1) >> 6) + 1) * CHUNK + LANES - 1) >> 7, nkb_max)
    lane = lax.broadcasted_iota(jnp.int32, (1, LANES), 1)
    limit = (((q0 + lane) >> 6) + 1) << 6
    sub = lax.broadcasted_iota(jnp.int32, (LANES, 1), 0)

    iqT = iqT_ref[...]
    qT = qT_ref[...]
    zpad = jnp.zeros((HEAD_DIM, LANES), BF16)
    rhs_idx = jnp.concatenate(
        [jnp.concatenate([iqT[h * IDX_DIM:(h + 1) * IDX_DIM, :], zpad], axis=0) for h in range(IDX_HEADS)],
        axis=1)
    wT = ikwT_ref[...]
    wrows = [wT[IDX_DIM + h:IDX_DIM + h + 1, :] * (IDX_SCALE * IDX_HEAD_SCALE) for h in range(IDX_HEADS)]

    def score_body(kb, carry):
        off = pl.multiple_of(kb * LANES, LANES)
        kk = ikw_ref[pl.ds(off, LANES), :].astype(BF16)
        s = _dot(kk, rhs_idx)
        acc = jnp.maximum(s[:, 0:LANES], 0.0) * wrows[0]
        for h in range(1, IDX_HEADS):
            acc = acc + jnp.maximum(s[:, h * LANES:(h + 1) * LANES], 0.0) * wrows[h]
        bits = pltpu.bitcast(acc, jnp.int32)
        key = bits ^ ((bits >> 31) & 0x7FFFFFFF)
        key = jnp.where(key == -1, 0, key)
        key = jnp.where(off + sub < limit, key, INT_MIN)
        sc_ref[pl.ds(off, LANES), :] = key
        return carry

    lax.fori_loop(0, nkb, score_body, 0)

    kf = float(ksel)

    def count(pred_fn):
        def body(kb, c):
            off = pl.multiple_of(kb * LANES, LANES)
            blk = sc_ref[pl.ds(off, LANES), :]
            return c + jnp.sum(jnp.where(pred_fn(blk), 1.0, 0.0), axis=0, keepdims=True)
        return lax.fori_loop(0, nkb, body, jnp.zeros((1, LANES), F32))

    def bit_body(i, t):
        cand = t + jnp.left_shift(jnp.int32(1), 31 - i)
        c = count(lambda blk: blk >= cand)
        return jnp.where(c >= kf, cand, t)

    thr = lax.fori_loop(0, 32, bit_body, jnp.full((1, LANES), INT_MIN, jnp.int32))
    c_gt = count(lambda blk: blk > thr)
    need = jnp.where(thr == INT_MIN, 0.0, kf - c_gt)

    def qpad(h):
        g = h // KV_GROUP
        qh = qT[h * HEAD_DIM:(h + 1) * HEAD_DIM, :]
        return jnp.concatenate([qh, zpad] if g == 0 else [zpad, qh], axis=0)

    rhs_q = [jnp.concatenate([qpad(g * KV_GROUP + hl) for hl in range(KV_GROUP)], axis=1)
             for g in range(N_KV_HEADS)]

    acc_ref[...] = jnp.zeros_like(acc_ref)

    def att_body(kb, carry):
        run_eq = carry[0]
        ms = carry[1:]
        off = pl.multiple_of(kb * LANES, LANES)
        blk = sc_ref[pl.ds(off, LANES), :]
        eqf = jnp.where(blk == thr, 1.0, 0.0)
        rank = run_eq + _dot(tri_ref[...], eqf.astype(BF16))
        selv = jnp.where(blk > thr, 1.0, jnp.where(rank < need, eqf, 0.0))
        sel = selv > 0.5
        run_eq = run_eq + jnp.sum(eqf, axis=0, keepdims=True)

        kb16 = kn_ref[pl.ds(off, LANES), :].astype(BF16)
        vT = v_ref[pl.ds(off, LANES), :].T
        new_ms = []
        for g in range(N_KV_HEADS):
            lg_all = _dot(kb16, rhs_q[g])
            in_g = (sub >= g * HEAD_DIM) & (sub < (g + 1) * HEAD_DIM)
            vaug = jnp.where(in_g, vT, 1.0).astype(BF16)
            ps, alphas = [], []
            for hl in range(KV_GROUP):
                h = g * KV_GROUP + hl
                lg = jnp.where(sel, lg_all[:, hl * LANES:(hl + 1) * LANES], NEG_INF)
                m_new = jnp.maximum(ms[h], jnp.max(lg, axis=0, keepdims=True))
                alphas.append(jnp.exp(ms[h] - m_new))
                ps.append(jnp.where(sel, jnp.exp(lg - m_new), 0.0).astype(BF16))
                new_ms.append(m_new)
            oT = _dot(vaug, jnp.concatenate(ps, axis=1))
            for hl in range(KV_GROUP):
                h = g * KV_GROUP + hl
                acc_ref[h] = acc_ref[h] * alphas[hl] + oT[:, hl * LANES:(hl + 1) * LANES]
        return (run_eq,) + tuple(new_ms)

    init = (jnp.zeros((1, LANES), F32),) + tuple(jnp.full((1, LANES), NEG_INF, F32) for _ in range(N_HEADS))
    lax.fori_loop(0, nkb, att_body, init)

    lane_lo = lane < HEAD_DIM
    norm = []
    for h in range(N_HEADS):
        aT = acc_ref[h].T
        norm.append(aT / pltpu.roll(aT, HEAD_DIM, 1))
    for c in range(N_HEADS // 2):
        a, b = norm[2 * c], norm[2 * c + 1]
        if (2 * c) // KV_GROUP == 0:
            blk = jnp.where(lane_lo, a, pltpu.roll(b, HEAD_DIM, 1))
        else:
            blk = jnp.where(lane_lo, pltpu.roll(a, HEAD_DIM, 1), b)
        out_ref[:, c * LANES:(c + 1) * LANES] = blk


def _attn_call(iqT, qT, ikwT, ikw_keys, kn_keys, v_keys, tri, pos0, ksel):
    B, _, T = qT.shape
    Lk = ikw_keys.shape[1]
    nq = T // LANES
    qblk = pl.BlockSpec((None, ATTN_WIDTH, LANES), lambda b, j: (b, 0, j))
    keys = pl.BlockSpec((None, Lk, LANES), lambda b, j: (b, 0, 0))
    kern = functools.partial(_attn_kernel, pos0=pos0, ksel=ksel, nkb_max=Lk // LANES)
    return pl.pallas_call(
        kern, grid=(B, nq),
        in_specs=[qblk, qblk, pl.BlockSpec((None, LANES, LANES), lambda b, j: (b, 0, j)),
                  keys, keys, keys, pl.BlockSpec((LANES, LANES), lambda b, j: (0, 0))],
        out_specs=pl.BlockSpec((None, LANES, ATTN_WIDTH), lambda b, j: (b, j, 0)),
        out_shape=jax.ShapeDtypeStruct((B, T, ATTN_WIDTH), F32),
        scratch_shapes=[pltpu.VMEM((Lk, LANES), jnp.int32), pltpu.VMEM((N_HEADS, LANES, LANES), F32)],
        compiler_params=pltpu.CompilerParams(dimension_semantics=("parallel", "arbitrary"),
                                             vmem_limit_bytes=VMEM_LIMIT),
        name="attn",
    )(iqT, qT, ikwT, ikw_keys, kn_keys, v_keys, tri)


def _ret_kernel(rq_ref, rk_ref, rv_ref, s0_ref, dmask_ref, zeta_ref, xi_ref, gch_ref, g_ref,
                o_ref, sfin_ref, st_ref, *, n_last):
    n = pl.program_id(1)

    @pl.when(n == 0)
    def _():
        st_ref[...] = s0_ref[...]

    for h in range(RET_HEADS):
        sl = slice(h * RET_DK, (h + 1) * RET_DK)
        q = rq_ref[:, sl]
        k = rk_ref[:, sl]
        v = rv_ref[:, sl]
        inner = _dot_nt(q, k) * dmask_ref[h]
        o = _dot(inner.astype(BF16), v)
        r = st_ref[h]
        o = o + _dot(q, r.astype(BF16)) * xi_ref[h]
        kz = k.astype(F32) * zeta_ref[h]
        u = _dot(kz.T.astype(BF16), v)
        st_ref[h] = gch_ref[h] * r + u
        ms = jnp.mean(o * o, axis=-1, keepdims=True)
        o_ref[:, sl] = (o * lax.rsqrt(ms + RMS_EPS)) * g_ref[:, sl]

    @pl.when(n == n_last)
    def _():
        sfin_ref[...] = st_ref[...]


def _ret_tables(c):
    log_g = jnp.log(1.0 - 2.0 ** (-5.0 - jnp.arange(RET_HEADS, dtype=F32)))
    i = jnp.arange(c, dtype=F32)
    diff = i[:, None] - i[None, :]
    dmask = jnp.where(diff[None] >= 0, jnp.exp(log_g[:, None, None] * jnp.maximum(diff, 0.0)[None]), 0.0)
    zeta = jnp.exp(log_g[:, None] * (c - 1 - i)[None, :])
    xi = jnp.exp(log_g[:, None] * (i + 1)[None, :])
    gch = jnp.exp(log_g * c)
    bc = lambda t: jnp.broadcast_to(t[:, :, None], (RET_HEADS, c, LANES))
    return dmask, bc(zeta), bc(xi), jnp.broadcast_to(gch[:, None, None], (RET_HEADS, 1, LANES))


def _ret_call(rq, rk, rv, s0, ret_g, t_real, c):
    B, T, _ = rq.shape
    nc = T // c
    dmask, zeta, xi, gch = _ret_tables(c)
    tok = pl.BlockSpec((None, c, RET_WIDTH), lambda b, n: (b, n, 0))
    st = pl.BlockSpec((None, RET_HEADS, RET_DK, RET_DV), lambda b, n: (b, 0, 0, 0))
    tab = lambda s: pl.BlockSpec(s, lambda b, n: (0,) * len(s))
    return pl.pallas_call(
        functools.partial(_ret_kernel, n_last=t_real // c - 1), grid=(B, nc),
        in_specs=[tok, tok, tok, st, tab((RET_HEADS, c, c)), tab((RET_HEADS, c, LANES)),
                  tab((RET_HEADS, c, LANES)), tab((RET_HEADS, 1, LANES)), tab((1, RET_WIDTH))],
        out_specs=(tok, st),
        out_shape=(jax.ShapeDtypeStruct((B, T, RET_WIDTH), F32),
                   jax.ShapeDtypeStruct((B, RET_HEADS, RET_DK, RET_DV), F32)),
        scratch_shapes=[pltpu.VMEM((RET_HEADS, RET_DK, RET_DV), F32)],
        compiler_params=pltpu.CompilerParams(dimension_semantics=("parallel", "arbitrary"),
                                             vmem_limit_bytes=VMEM_LIMIT),
        name="ret",
    )(rq, rk, rv, s0, dmask, zeta, xi, gch, ret_g)


def _out_kernel(h_ref, attn_ref, ag_ref, ret_ref, rg_ref, p_ref, wo_ref, pg_ref, pp_ref, png_ref, o_ref):
    ag = ag_ref[...]
    rg = rg_ref[...]
    mix_a = (attn_ref[...] * (ag * jax.nn.sigmoid(ag))).astype(BF16)
    mix_r = (ret_ref[...] * (rg * jax.nn.sigmoid(rg))).astype(BF16)
    h1 = h_ref[...] + _dot(mix_a, wo_ref[0:ATTN_WIDTH, :]) + _dot(mix_r, wo_ref[ATTN_WIDTH:, :])
    ms = jnp.mean(h1 * h1, axis=-1, keepdims=True)
    hn = ((h1 * lax.rsqrt(ms + RMS_EPS)) * png_ref[...]).astype(BF16)
    gate = jax.nn.sigmoid(_dot(hn, pg_ref[...]))
    o_ref[...] = h1 + gate * _dot(p_ref[...].astype(BF16), pp_ref[...])


def _out_call(h, attn, ag, ret, rg, p, wo, pgate, pproj, png, tm):
    B, T, _ = h.shape
    tok = lambda width: pl.BlockSpec((None, tm, width), lambda b, i: (b, i, 0))
    const = lambda s: pl.BlockSpec(s, lambda b, i: (0, 0))
    return pl.pallas_call(
        _out_kernel, grid=(B, T // tm),
        in_specs=[tok(D_MODEL), tok(ATTN_WIDTH), tok(ATTN_WIDTH), tok(RET_WIDTH), tok(RET_WIDTH), tok(PLE_DIM),
                  const((D_MODEL, D_MODEL)), const((D_MODEL, D_MODEL)), const((PLE_DIM, D_MODEL)),
                  const((1, D_MODEL))],
        out_specs=tok(D_MODEL),
        out_shape=jax.ShapeDtypeStruct((B, T, D_MODEL), F32),
        compiler_params=pltpu.CompilerParams(dimension_semantics=("parallel", "parallel"),
                                             vmem_limit_bytes=VMEM_LIMIT),
        name="out",
    )(h, attn, ag, ret, rg, p, wo, pgate, pproj, png)


def _rope_tables(pos0, t):
    half = RET_DK // 2
    inv = ROPE_BASE ** (-jnp.arange(half, dtype=F32) / half)
    ang = (pos0 + jnp.arange(t, dtype=jnp.int32)).astype(F32)[:, None] * inv[None, :]
    cos, sin = jnp.cos(ang), jnp.sin(ang)
    return jnp.concatenate([cos, cos], axis=1), jnp.concatenate([-sin, sin], axis=1)


def _group_mean_matrix(width, group):
    i = jnp.arange(width) // group
    return jnp.where(i[:, None] == i[None, :], 1.0 / group, 0.0).astype(BF16)


def _layer(h, p, pos0, t_real, past, s0, lw, consts, tm, chunk):
    w, wo, ng, qg, kg, ret_g, pproj, pgate, png = lw
    gq, gk, tri = consts
    B, T, _ = h.shape
    cosf, sinf = _rope_tables(pos0, T)
    qT, kn, v, ag, iqT, ikw, ikwT, rq, rk, rv, rg = _proj_call(h, ng, w, qg, kg, gq, gk, cosf, sinf, tm)
    if past is None:
        ikw_keys, kn_keys, v_keys, n_past = ikw, kn, v, 0
    else:
        past_k, past_v, past_ikw = past
        n_past = past_k.shape[1]
        kn_keys = jnp.concatenate([past_k, kn], axis=1)
        v_keys = jnp.concatenate([past_v, v], axis=1)
        ikw_keys = jnp.concatenate([past_ikw, ikw], axis=1)
    ksel = min(TOPK_MAX, (n_past + t_real) // 4)
    attn = _attn_call(iqT, qT, ikwT, ikw_keys, kn_keys, v_keys, tri, pos0, ksel)
    ret, s_new = _ret_call(rq, rk, rv, s0, ret_g, t_real, chunk)
    h_new = _out_call(h, attn, ag, ret, rg, p, wo, pgate, pproj, png, tm)
    return h_new, kn, v, ikw, s_new


def _permute_w_in(w_in):
    depth = w_in.shape[0]
    a = w_in[:, :, 0:1280]
    iq = w_in[:, :, 1280:1792]
    ik_iw = w_in[:, :, 1792:1864]
    r = w_in[:, :, 1864:3912]
    pad = jnp.zeros((depth, D_MODEL, W_COLS - C_IKW - 72), w_in.dtype)
    return jnp.concatenate([a, iq, r, ik_iw, pad], axis=2).astype(BF16)


def kernel(x_prompt, x_sample, cache_k, cache_v, cache_k_idx, state_ret, p_prompt, p_sample,
           w_in, w_out, norm_g, q_norm_g, k_norm_g, ret_norm_g, ple_proj, ple_gate, ple_norm_g):
    depth = w_in.shape[0]
    b_p, t_p, _ = x_prompt.shape
    b_s, t_s, _ = x_sample.shape
    past = cache_k.shape[2]
    t_s_pad = -(-t_s // LANES) * LANES

    w_perm = _permute_w_in(w_in)
    wo = w_out.astype(BF16)
    pproj = ple_proj.astype(BF16)
    pgate = ple_gate.astype(BF16)
    qg = jnp.tile(q_norm_g, (1, N_HEADS))[:, None, :]
    kg = jnp.tile(k_norm_g, (1, N_KV_HEADS))[:, None, :]
    consts = (_group_mean_matrix(ATTN_WIDTH, HEAD_DIM), _group_mean_matrix(KV_WIDTH, HEAD_DIM),
              (jnp.arange(LANES)[None, :] < jnp.arange(LANES)[:, None]).astype(BF16))

    pad_t = lambda a, axis: jnp.pad(a, [(0, t_s_pad - t_s) if d == axis else (0, 0) for d in range(a.ndim)])
    h_p = x_prompt
    h_s = pad_t(x_sample, 1)
    p_s = pad_t(p_sample, 2)
    r0 = jnp.zeros((b_p, RET_HEADS, RET_DK, RET_DV), F32)
    past_k = cache_k.reshape(depth, b_s, past, KV_WIDTH)
    past_v = cache_v.reshape(depth, b_s, past, KV_WIDTH)
    past_ikw = jnp.pad(cache_k_idx, ((0, 0), (0, 0), (0, 0), (0, LANES - IDX_DIM)))

    outs = [[] for _ in range(8)]
    for i in range(depth):
        lw = (w_perm[i], wo[i], norm_g[i][None], qg[i], kg[i], ret_norm_g[i][None], pproj[i], pgate[i],
              ple_norm_g[i][None])
        h_p, kp, vp, ikwp, rp = _layer(h_p, p_prompt[i], 0, t_p, None, r0, lw, consts, 512, 2 * CHUNK)
        h_s, ks, vs, ikws, rs = _layer(h_s, p_s[i], past, t_s, (past_k[i], past_v[i], past_ikw[i]),
                                       state_ret[i], lw, consts, LANES, CHUNK)
        new = (kp.reshape(b_p, t_p, N_KV_HEADS, HEAD_DIM), vp.reshape(b_p, t_p, N_KV_HEADS, HEAD_DIM),
               ikwp[:, :, :IDX_DIM], rp,
               ks[:, :t_s].reshape(b_s, t_s, N_KV_HEADS, HEAD_DIM), vs[:, :t_s].reshape(b_s, t_s, N_KV_HEADS, HEAD_DIM),
               ikws[:, :t_s, :IDX_DIM], rs)
        for o, a in zip(outs, new):
            o.append(a)
    stacked = [jnp.stack(o) for o in outs]
    return (h_p, h_s[:, :t_s], *stacked)
```

```python
import functools

import jax
import jax.numpy as jnp
from jax import lax
from jax.experimental import pallas as pl
from jax.experimental.pallas import tpu as pltpu

F32 = jnp.float32
BF16 = jnp.bfloat16

D_MODEL = 1024
CHUNK = 64
ATTN_WIDTH = 512
RET_WIDTH = 512
N_HEADS = 8
HEAD_DIM = 64
N_KV_HEADS = 2
KV_GROUP = N_HEADS // N_KV_HEADS
KV_WIDTH = N_KV_HEADS * HEAD_DIM
IDX_HEADS = 8
IDX_DIM = 64
TOPK_MAX = 256
RET_HEADS = 4
RET_DK = 128
RET_DV = 128
PLE_DIM = 256
RMS_EPS = 1e-6
NEG_INF = -1e30
ATTN_SCALE = HEAD_DIM ** -0.5
IDX_SCALE = IDX_DIM ** -0.5
IDX_HEAD_SCALE = IDX_HEADS ** -0.5
ROPE_BASE = 10000.0
LOG2_E = 1.4426950408889634

LANES = 128
SUBLANES = 8
KEY_BLOCK = 256
DEN_ROWS = 16
INT_MIN = -(2 ** 31)
I16_MIN = -(2 ** 15)
I16_ROWS = 16
VMEM_LIMIT = 56 * 1024 * 1024

C_Q = 0
C_K = 512
C_V = 640
C_AG = 768
C_IQ = 1280
C_RQ = 1792
C_RK = 2304
C_RV = 2816
C_RG = 3328
C_IKW = 3840
W_COLS = 3968


def _dot(a, b):
    return jnp.dot(a, b, preferred_element_type=F32)


def _dot_nt(a, b):
    return lax.dot_general(a, b, (((1,), (1,)), ((), ())), preferred_element_type=F32)


def _proj_kernel(x_ref, ng_ref, w_ref, qg_ref, kg_ref, gq_ref, gk_ref, cos_ref, sin_ref,
                 qT_ref, kn_ref, v_ref, ag_ref, iqT_ref, ikw_ref, ikwT_ref,
                 rq_ref, rk_ref, rv_ref, rg_ref):
    x = x_ref[...]
    ms = jnp.mean(x * x, axis=-1, keepdims=True)
    xn = ((x * lax.rsqrt(ms + RMS_EPS)) * ng_ref[...]).astype(BF16)

    def proj(c0, c1):
        return _dot(xn, w_ref[:, c0:c1])

    def head_ms(u, gmat_ref):
        sq = u * u
        hi = sq.astype(BF16)
        lo = (sq - hi.astype(F32)).astype(BF16)
        return _dot(hi, gmat_ref[...]) + _dot(lo, gmat_ref[...])

    uq = proj(C_Q, C_K)
    qn = (uq * lax.rsqrt(head_ms(uq, gq_ref) + RMS_EPS)) * qg_ref[...]
    qT_ref[...] = (qn * (ATTN_SCALE * LOG2_E)).T.astype(BF16)

    uk = proj(C_K, C_V)
    kn_ref[...] = (uk * lax.rsqrt(head_ms(uk, gk_ref) + RMS_EPS)) * kg_ref[...]
    v_ref[...] = proj(C_V, C_AG)
    ag_ref[...] = proj(C_AG, C_IQ)
    iqT_ref[...] = proj(C_IQ, C_RQ).T.astype(BF16)
    ikw = proj(C_IKW, W_COLS)
    ikw_ref[...] = ikw
    ikwT_ref[...] = ikw.T

    cosf = cos_ref[...]
    sinf = sin_ref[...]

    def rotary(u):
        parts = []
        for h in range(RET_HEADS):
            xh = u[:, h * RET_DK:(h + 1) * RET_DK]
            parts.append(xh * cosf + pltpu.roll(xh, RET_DK // 2, 1) * sinf)
        return jnp.concatenate(parts, axis=1)

    rq_ref[...] = rotary(proj(C_RQ, C_RK)).astype(BF16)
    rk_ref[...] = (rotary(proj(C_RK, C_RV)) * (RET_DK ** -0.5)).astype(BF16)
    rv_ref[...] = proj(C_RV, C_RG).astype(BF16)
    rg_ref[...] = proj(C_RG, C_IKW)


def _proj_call(x, ng, w, qg, kg, gq, gk, cosf, sinf, tm):
    B, T, _ = x.shape
    nt = T // tm
    row = lambda b, i: (b, i, 0)
    col = lambda b, i: (b, 0, i)
    const = lambda b, i: (0, 0)
    tok = lambda width: pl.BlockSpec((None, tm, width), row)
    tokT = lambda width: pl.BlockSpec((None, width, tm), col)
    out_shape = (
        jax.ShapeDtypeStruct((B, ATTN_WIDTH, T), BF16),
        jax.ShapeDtypeStruct((B, T, KV_WIDTH), F32),
        jax.ShapeDtypeStruct((B, T, KV_WIDTH), F32),
        jax.ShapeDtypeStruct((B, T, ATTN_WIDTH), F32),
        jax.ShapeDtypeStruct((B, ATTN_WIDTH, T), BF16),
        jax.ShapeDtypeStruct((B, T, LANES), F32),
        jax.ShapeDtypeStruct((B, LANES, T), F32),
        jax.ShapeDtypeStruct((B, T, RET_WIDTH), BF16),
        jax.ShapeDtypeStruct((B, T, RET_WIDTH), BF16),
        jax.ShapeDtypeStruct((B, T, RET_WIDTH), BF16),
        jax.ShapeDtypeStruct((B, T, RET_WIDTH), F32),
    )
    out_specs = (tokT(ATTN_WIDTH), tok(KV_WIDTH), tok(KV_WIDTH), tok(ATTN_WIDTH), tokT(ATTN_WIDTH),
                 tok(LANES), tokT(LANES), tok(RET_WIDTH), tok(RET_WIDTH), tok(RET_WIDTH), tok(RET_WIDTH))
    in_specs = [
        tok(D_MODEL),
        pl.BlockSpec((1, D_MODEL), const),
        pl.BlockSpec((D_MODEL, W_COLS), const),
        pl.BlockSpec((1, ATTN_WIDTH), const),
        pl.BlockSpec((1, KV_WIDTH), const),
        pl.BlockSpec((ATTN_WIDTH, ATTN_WIDTH), const),
        pl.BlockSpec((KV_WIDTH, KV_WIDTH), const),
        pl.BlockSpec((tm, LANES), lambda b, i: (i, 0)),
        pl.BlockSpec((tm, LANES), lambda b, i: (i, 0)),
    ]
    return pl.pallas_call(
        _proj_kernel, grid=(B, nt), in_specs=in_specs, out_specs=out_specs, out_shape=out_shape,
        compiler_params=pltpu.CompilerParams(dimension_semantics=("parallel", "parallel"),
                                             vmem_limit_bytes=VMEM_LIMIT),
        name="proj",
    )(x, ng, w, qg, kg, gq, gk, cosf, sinf)


def _attn_kernel(iqT_ref, qT_ref, ikwT_ref, ikw_ref, kn_ref, v_ref, tri_ref, out_ref,
                 sc_ref, hi_ref, lo_ref, lg_ref, acc_ref, *, pos0, ksel, nkb_max, lq):
    q0 = pos0 + pl.program_id(1) * lq
    lane = lax.broadcasted_iota(jnp.int32, (1, lq), 1)
    limit = (((q0 + lane) >> 6) + 1) << 6
    last_limit = (((q0 + lq - 1) >> 6) + 1) << 6
    nkb = jnp.minimum((last_limit + KEY_BLOCK - 1) // KEY_BLOCK, nkb_max)
    sub = lax.broadcasted_iota(jnp.int32, (KEY_BLOCK, 1), 0)

    iqT = iqT_ref[...]
    qT = qT_ref[...]
    zpad = jnp.zeros((HEAD_DIM, lq), BF16)
    rhs_idx = [jnp.concatenate([iqT[h * IDX_DIM:(h + 1) * IDX_DIM, :], zpad], axis=0)
               for h in range(IDX_HEADS)]
    wT = ikwT_ref[...]
    wrows = [wT[IDX_DIM + h:IDX_DIM + h + 1, :] * (IDX_SCALE * IDX_HEAD_SCALE) for h in range(IDX_HEADS)]

    def score_body(kb, carry):
        off = pl.multiple_of(kb * KEY_BLOCK, KEY_BLOCK)
        kk = ikw_ref[pl.ds(off, KEY_BLOCK), :].astype(BF16)
        acc = jnp.maximum(_dot(kk, rhs_idx[0]), 0.0) * wrows[0]
        for h in range(1, IDX_HEADS):
            acc = acc + jnp.maximum(_dot(kk, rhs_idx[h]), 0.0) * wrows[h]
        bits = pltpu.bitcast(acc, jnp.int32)
        key = bits ^ ((bits >> 31) & 0x7FFFFFFF)
        key = jnp.where(key == -1, 0, key)
        key = jnp.where(off + sub < limit, key, INT_MIN)
        sc_ref[pl.ds(off, KEY_BLOCK), :] = key
        hi_ref[pl.ds(off, KEY_BLOCK), :] = (key >> 16).astype(jnp.int16)
        lo_ref[pl.ds(off, KEY_BLOCK), :] = ((key & 0xFFFF) + I16_MIN).astype(jnp.int16)
        return carry

    lax.fori_loop(0, nkb, score_body, 0)

    kf = float(ksel)

    def count16(ref, pred_fn):
        def body(kb, c):
            off = pl.multiple_of(kb * KEY_BLOCK, KEY_BLOCK)
            hit = jnp.where(pred_fn(ref[pl.ds(off, KEY_BLOCK), :]), jnp.int16(1), jnp.int16(0))
            parts = [hit[r:r + I16_ROWS, :] for r in range(0, KEY_BLOCK, I16_ROWS)]
            while len(parts) > 1:
                parts = [parts[r] + parts[r + 1] for r in range(0, len(parts), 2)]
            return c + parts[0]
        part = lax.fori_loop(0, nkb, body, jnp.zeros((I16_ROWS, lq), jnp.int16))
        return jnp.sum(part.astype(F32), axis=0, keepdims=True)

    def kth_largest16(ref, k_row):
        def bit_body(i, t):
            cand = t + jnp.left_shift(jnp.int32(1), 15 - i)
            cand16 = cand.astype(jnp.int16)
            c = count16(ref, lambda blk: blk >= cand16)
            return jnp.where(c >= k_row, cand, t)
        return lax.fori_loop(0, 16, bit_body, jnp.full((1, lq), I16_MIN, jnp.int32))

    t_hi = kth_largest16(hi_ref, kf)
    t_hi16 = t_hi.astype(jnp.int16)
    k_lo = kf - count16(hi_ref, lambda blk: blk > t_hi16)

    def plane_body(kb, carry):
        off = pl.multiple_of(kb * KEY_BLOCK, KEY_BLOCK)
        lo_ref[pl.ds(off, KEY_BLOCK), :] = jnp.where(
            hi_ref[pl.ds(off, KEY_BLOCK), :] == t_hi16, lo_ref[pl.ds(off, KEY_BLOCK), :], jnp.int16(I16_MIN))
        return carry

    lax.fori_loop(0, nkb, plane_body, 0)
    t_lo = kth_largest16(lo_ref, k_lo)
    t_lo16 = t_lo.astype(jnp.int16)
    thr = (t_hi << 16) | (t_lo - I16_MIN)
    c_gt_lo = count16(lo_ref, lambda blk: blk > t_lo16)
    need = jnp.where(thr == INT_MIN, 0.0, k_lo - c_gt_lo)

    def qpad(h):
        qh = qT[h * HEAD_DIM:(h + 1) * HEAD_DIM, :]
        return jnp.concatenate([qh, zpad] if h // KV_GROUP == 0 else [zpad, qh], axis=0)

    rhs_q = [qpad(h) for h in range(N_HEADS)]

    def logit_body(kb, carry):
        run_eq = carry[0]
        off = pl.multiple_of(kb * KEY_BLOCK, KEY_BLOCK)
        blk = sc_ref[pl.ds(off, KEY_BLOCK), :]
        eqf = jnp.where(blk == thr, 1.0, 0.0)
        rank = run_eq + _dot(tri_ref[...], eqf.astype(BF16))
        sel = jnp.where(blk > thr, 1.0, jnp.where(rank < need, eqf, 0.0)) > 0.5
        run_eq = run_eq + jnp.sum(eqf, axis=0, keepdims=True)
        kb16 = kn_ref[pl.ds(off, KEY_BLOCK), :].astype(BF16)
        new_max = []
        for h in range(N_HEADS):
            lg = jnp.where(sel, _dot(kb16, rhs_q[h]), NEG_INF)
            lg_ref[h, pl.ds(off, KEY_BLOCK), :] = lg
            new_max.append(jnp.maximum(
                carry[1 + h], jnp.max(lg.reshape(KEY_BLOCK // SUBLANES, SUBLANES, lq), axis=0)))
        return (run_eq,) + tuple(new_max)

    init = (jnp.zeros((1, lq), F32),) + tuple(jnp.full((SUBLANES, lq), NEG_INF, F32) for _ in range(N_HEADS))
    carry = lax.fori_loop(0, nkb, logit_body, init)
    ms = [jnp.max(c, axis=0, keepdims=True) for c in carry[1:]]

    ones = jnp.ones((DEN_ROWS, KEY_BLOCK), F32)
    acc_ref[...] = jnp.zeros_like(acc_ref)

    def pv_body(kb, carry):
        off = pl.multiple_of(kb * KEY_BLOCK, KEY_BLOCK)
        vT = v_ref[pl.ds(off, KEY_BLOCK), :].T
        vaug = [jnp.concatenate([vT[g * HEAD_DIM:(g + 1) * HEAD_DIM, :], ones], axis=0).astype(BF16)
                for g in range(N_KV_HEADS)]
        for h in range(N_HEADS):
            p = jnp.exp2(lg_ref[h, pl.ds(off, KEY_BLOCK), :] - ms[h]).astype(BF16)
            acc_ref[h] += _dot(vaug[h // KV_GROUP], p)
        return carry

    lax.fori_loop(0, nkb, pv_body, 0)

    outs = []
    for h in range(N_HEADS):
        a = acc_ref[h]
        outs.append(a[0:HEAD_DIM, :] * (1.0 / a[HEAD_DIM:HEAD_DIM + 1, :]))
    out_ref[...] = jnp.concatenate(outs, axis=0).T


def _attn_call(iqT, qT, ikwT, ikw_keys, kn_keys, v_keys, tri, pos0, ksel, lq):
    B, _, T = qT.shape
    Lk = ikw_keys.shape[1]
    qblk = lambda rows: pl.BlockSpec((None, rows, lq), lambda b, j: (b, 0, j))
    keys = pl.BlockSpec((None, Lk, LANES), lambda b, j: (b, 0, 0))
    kern = functools.partial(_attn_kernel, pos0=pos0, ksel=ksel, nkb_max=Lk // KEY_BLOCK, lq=lq)
    return pl.pallas_call(
        kern, grid=(B, T // lq),
        in_specs=[qblk(ATTN_WIDTH), qblk(ATTN_WIDTH), qblk(LANES), keys, keys, keys,
                  pl.BlockSpec((KEY_BLOCK, KEY_BLOCK), lambda b, j: (0, 0))],
        out_specs=pl.BlockSpec((None, lq, ATTN_WIDTH), lambda b, j: (b, j, 0)),
        out_shape=jax.ShapeDtypeStruct((B, T, ATTN_WIDTH), F32),
        scratch_shapes=[pltpu.VMEM((Lk, lq), jnp.int32),
                        pltpu.VMEM((Lk, lq), jnp.int16),
                        pltpu.VMEM((Lk, lq), jnp.int16),
                        pltpu.VMEM((N_HEADS, Lk, lq), F32),
                        pltpu.VMEM((N_HEADS, HEAD_DIM + DEN_ROWS, lq), F32)],
        compiler_params=pltpu.CompilerParams(dimension_semantics=("parallel", "arbitrary"),
                                             vmem_limit_bytes=VMEM_LIMIT),
        name="attn",
    )(iqT, qT, ikwT, ikw_keys, kn_keys, v_keys, tri)


def _ret_kernel(rq_ref, rk_ref, rv_ref, s0_ref, dmask_ref, zeta_ref, xi_ref, gch_ref, g_ref,
                o_ref, sfin_ref, st_ref, *, n_last):
    n = pl.program_id(1)

    @pl.when(n == 0)
    def _():
        st_ref[...] = s0_ref[...]

    for h in range(RET_HEADS):
        sl = slice(h * RET_DK, (h + 1) * RET_DK)
        q = rq_ref[:, sl]
        k = rk_ref[:, sl]
        v = rv_ref[:, sl]
        inner = _dot_nt(q, k) * dmask_ref[h]
        o = _dot(inner.astype(BF16), v)
        r = st_ref[h]
        o = o + _dot(q, r.astype(BF16)) * xi_ref[h]
        kz = k.astype(F32) * zeta_ref[h]
        u = _dot(kz.T.astype(BF16), v)
        st_ref[h] = gch_ref[h] * r + u
        ms = jnp.mean(o * o, axis=-1, keepdims=True)
        o_ref[:, sl] = (o * lax.rsqrt(ms + RMS_EPS)) * g_ref[:, sl]

    @pl.when(n == n_last)
    def _():
        sfin_ref[...] = st_ref[...]


def _ret_tables(c):
    log_g = jnp.log(1.0 - 2.0 ** (-5.0 - jnp.arange(RET_HEADS, dtype=F32)))
    i = jnp.arange(c, dtype=F32)
    diff = i[:, None] - i[None, :]
    dmask = jnp.where(diff[None] >= 0, jnp.exp(log_g[:, None, None] * jnp.maximum(diff, 0.0)[None]), 0.0)
    zeta = jnp.exp(log_g[:, None] * (c - 1 - i)[None, :])
    xi = jnp.exp(log_g[:, None] * (i + 1)[None, :])
    gch = jnp.exp(log_g * c)
    bc = lambda t: jnp.broadcast_to(t[:, :, None], (RET_HEADS, c, LANES))
    return dmask, bc(zeta), bc(xi), jnp.broadcast_to(gch[:, None, None], (RET_HEADS, 1, LANES))


def _ret_call(rq, rk, rv, s0, ret_g, t_real, c):
    B, T, _ = rq.shape
    nc = T // c
    dmask, zeta, xi, gch = _ret_tables(c)
    tok = pl.BlockSpec((None, c, RET_WIDTH), lambda b, n: (b, n, 0))
    st = pl.BlockSpec((None, RET_HEADS, RET_DK, RET_DV), lambda b, n: (b, 0, 0, 0))
    tab = lambda s: pl.BlockSpec(s, lambda b, n: (0,) * len(s))
    return pl.pallas_call(
        functools.partial(_ret_kernel, n_last=t_real // c - 1), grid=(B, nc),
        in_specs=[tok, tok, tok, st, tab((RET_HEADS, c, c)), tab((RET_HEADS, c, LANES)),
                  tab((RET_HEADS, c, LANES)), tab((RET_HEADS, 1, LANES)), tab((1, RET_WIDTH))],
        out_specs=(tok, st),
        out_shape=(jax.ShapeDtypeStruct((B, T, RET_WIDTH), F32),
                   jax.ShapeDtypeStruct((B, RET_HEADS, RET_DK, RET_DV), F32)),
        scratch_shapes=[pltpu.VMEM((RET_HEADS, RET_DK, RET_DV), F32)],
        compiler_params=pltpu.CompilerParams(dimension_semantics=("parallel", "arbitrary"),
                                             vmem_limit_bytes=VMEM_LIMIT),
        name="ret",
    )(rq, rk, rv, s0, dmask, zeta, xi, gch, ret_g)


def _out_kernel(h_ref, attn_ref, ag_ref, ret_ref, rg_ref, p_ref, wo_ref, pg_ref, pp_ref, png_ref, o_ref):
    ag = ag_ref[...]
    rg = rg_ref[...]
    mix_a = (attn_ref[...] * (ag * jax.nn.sigmoid(ag))).astype(BF16)
    mix_r = (ret_ref[...] * (rg * jax.nn.sigmoid(rg))).astype(BF16)
    h1 = h_ref[...] + _dot(mix_a, wo_ref[0:ATTN_WIDTH, :]) + _dot(mix_r, wo_ref[ATTN_WIDTH:, :])
    ms = jnp.mean(h1 * h1, axis=-1, keepdims=True)
    hn = ((h1 * lax.rsqrt(ms + RMS_EPS)) * png_ref[...]).astype(BF16)
    gate = jax.nn.sigmoid(_dot(hn, pg_ref[...]))
    o_ref[...] = h1 + gate * _dot(p_ref[...].astype(BF16), pp_ref[...])


def _out_call(h, attn, ag, ret, rg, p, wo, pgate, pproj, png, tm):
    B, T, _ = h.shape
    tok = lambda width: pl.BlockSpec((None, tm, width), lambda b, i: (b, i, 0))
    const = lambda s: pl.BlockSpec(s, lambda b, i: (0, 0))
    return pl.pallas_call(
        _out_kernel, grid=(B, T // tm),
        in_specs=[tok(D_MODEL), tok(ATTN_WIDTH), tok(ATTN_WIDTH), tok(RET_WIDTH), tok(RET_WIDTH), tok(PLE_DIM),
                  const((D_MODEL, D_MODEL)), const((D_MODEL, D_MODEL)), const((PLE_DIM, D_MODEL)),
                  const((1, D_MODEL))],
        out_specs=tok(D_MODEL),
        out_shape=jax.ShapeDtypeStruct((B, T, D_MODEL), F32),
        compiler_params=pltpu.CompilerParams(dimension_semantics=("parallel", "parallel"),
                                             vmem_limit_bytes=VMEM_LIMIT),
        name="out",
    )(h, attn, ag, ret, rg, p, wo, pgate, pproj, png)


def _rope_tables(pos0, t):
    half = RET_DK // 2
    inv = ROPE_BASE ** (-jnp.arange(half, dtype=F32) / half)
    ang = (pos0 + jnp.arange(t, dtype=jnp.int32)).astype(F32)[:, None] * inv[None, :]
    cos, sin = jnp.cos(ang), jnp.sin(ang)
    return jnp.concatenate([cos, cos], axis=1), jnp.concatenate([-sin, sin], axis=1)


def _group_mean_matrix(width, group):
    i = jnp.arange(width) // group
    return jnp.where(i[:, None] == i[None, :], 1.0 / group, 0.0).astype(BF16)


def _strict_lower(n):
    return (jnp.arange(n)[None, :] < jnp.arange(n)[:, None]).astype(BF16)


def _layer(h, p, pos0, t_real, past, s0, lw, consts, tm, chunk, lq):
    w, wo, ng, qg, kg, ret_g, pproj, pgate, png = lw
    gq, gk, tri = consts
    B, T, _ = h.shape
    cosf, sinf = _rope_tables(pos0, T)
    qT, kn, v, ag, iqT, ikw, ikwT, rq, rk, rv, rg = _proj_call(h, ng, w, qg, kg, gq, gk, cosf, sinf, tm)
    n_past = 0 if past is None else past[0].shape[1]
    lk = -(-(n_past + T) // KEY_BLOCK) * KEY_BLOCK
    if lk == T:
        ikw_keys, kn_keys, v_keys = ikw, kn, v
    else:
        tail = jnp.zeros((B, lk - n_past - T, LANES), F32)
        cat = lambda old, new: jnp.concatenate(([] if past is None else [old]) + [new, tail], axis=1)
        past_k, past_v, past_ikw = (None, None, None) if past is None else past
        kn_keys, v_keys, ikw_keys = cat(past_k, kn), cat(past_v, v), cat(past_ikw, ikw)
    ksel = min(TOPK_MAX, (n_past + t_real) // 4)
    attn = _attn_call(iqT, qT, ikwT, ikw_keys, kn_keys, v_keys, tri, pos0, ksel, lq)
    ret, s_new = _ret_call(rq, rk, rv, s0, ret_g, t_real, chunk)
    h_new = _out_call(h, attn, ag, ret, rg, p, wo, pgate, pproj, png, tm)
    return h_new, kn, v, ikw, s_new


def _permute_w_in(w_in):
    depth = w_in.shape[0]
    a = w_in[:, :, 0:1280]
    iq = w_in[:, :, 1280:1792]
    ik_iw = w_in[:, :, 1792:1864]
    r = w_in[:, :, 1864:3912]
    pad = jnp.zeros((depth, D_MODEL, W_COLS - C_IKW - 72), w_in.dtype)
    return jnp.concatenate([a, iq, r, ik_iw, pad], axis=2).astype(BF16)


def kernel(x_prompt, x_sample, cache_k, cache_v, cache_k_idx, state_ret, p_prompt, p_sample,
           w_in, w_out, norm_g, q_norm_g, k_norm_g, ret_norm_g, ple_proj, ple_gate, ple_norm_g):
    depth = w_in.shape[0]
    b_p, t_p, _ = x_prompt.shape
    b_s, t_s, _ = x_sample.shape
    past = cache_k.shape[2]
    t_s_pad = -(-t_s // LANES) * LANES

    w_perm = _permute_w_in(w_in)
    wo = w_out.astype(BF16)
    pproj = ple_proj.astype(BF16)
    pgate = ple_gate.astype(BF16)
    qg = jnp.tile(q_norm_g, (1, N_HEADS))[:, None, :]
    kg = jnp.tile(k_norm_g, (1, N_KV_HEADS))[:, None, :]
    consts = (_group_mean_matrix(ATTN_WIDTH, HEAD_DIM), _group_mean_matrix(KV_WIDTH, HEAD_DIM),
              _strict_lower(KEY_BLOCK))

    pad_t = lambda a, axis: jnp.pad(a, [(0, t_s_pad - t_s) if d == axis else (0, 0) for d in range(a.ndim)])
    h_p = x_prompt
    h_s = pad_t(x_sample, 1)
    p_s = pad_t(p_sample, 2)
    r0 = jnp.zeros((b_p, RET_HEADS, RET_DK, RET_DV), F32)
    past_k = cache_k.reshape(depth, b_s, past, KV_WIDTH)
    past_v = cache_v.reshape(depth, b_s, past, KV_WIDTH)
    past_ikw = jnp.pad(cache_k_idx, ((0, 0), (0, 0), (0, 0), (0, LANES - IDX_DIM)))

    outs = [[] for _ in range(8)]
    for i in range(depth):
        lw = (w_perm[i], wo[i], norm_g[i][None], qg[i], kg[i], ret_norm_g[i][None], pproj[i], pgate[i],
              ple_norm_g[i][None])
        h_p, kp, vp, ikwp, rp = _layer(h_p, p_prompt[i], 0, t_p, None, r0, lw, consts, 512, 2 * CHUNK, 2 * LANES)
        h_s, ks, vs, ikws, rs = _layer(h_s, p_s[i], past, t_s, (past_k[i], past_v[i], past_ikw[i]),
                                       state_ret[i], lw, consts, LANES, CHUNK, LANES)
        new = (kp.reshape(b_p, t_p, N_KV_HEADS, HEAD_DIM), vp.reshape(b_p, t_p, N_KV_HEADS, HEAD_DIM),
               ikwp[:, :, :IDX_DIM], rp,
               ks[:, :t_s].reshape(b_s, t_s, N_KV_HEADS, HEAD_DIM), vs[:, :t_s].reshape(b_s, t_s, N_KV_HEADS, HEAD_DIM),
               ikws[:, :t_s, :IDX_DIM], rs)
        for o, a in zip(outs, new):
            o.append(a)
    stacked = [jnp.stack(o) for o in outs]
    return (h_p, h_s[:, :t_s], *stacked)
```

```python
import functools

import jax
import jax.numpy as jnp
from jax import lax
from jax.experimental import pallas as pl
from jax.experimental.pallas import tpu as pltpu

F32 = jnp.float32
BF16 = jnp.bfloat16

D_MODEL = 1024
CHUNK = 64
ATTN_WIDTH = 512
RET_WIDTH = 512
N_HEADS = 8
HEAD_DIM = 64
N_KV_HEADS = 2
KV_GROUP = N_HEADS // N_KV_HEADS
KV_WIDTH = N_KV_HEADS * HEAD_DIM
IDX_HEADS = 8
IDX_DIM = 64
TOPK_MAX = 256
RET_HEADS = 4
RET_DK = 128
RET_DV = 128
PLE_DIM = 256
RMS_EPS = 1e-6
NEG_INF = -1e30
ATTN_SCALE = HEAD_DIM ** -0.5
IDX_SCALE = IDX_DIM ** -0.5
IDX_HEAD_SCALE = IDX_HEADS ** -0.5
ROPE_BASE = 10000.0
LOG2_E = 1.4426950408889634

LANES = 128
SUBLANES = 8
KEY_BLOCK = 256
PROMPT_ROWS = 512
RET_BATCH_ROWS = 4
DEN_ROWS = 16
INT_MIN = -(2 ** 31)
I16_MIN = -(2 ** 15)
I16_ROWS = 16
VMEM_LIMIT = 56 * 1024 * 1024

C_Q = 0
C_K = 512
C_V = 640
C_AG = 768
C_IQ = 1280
C_RQ = 1792
C_RK = 2304
C_RV = 2816
C_RG = 3328
C_IKW = 3840
W_COLS = 3968


def _dot(a, b):
    return jnp.dot(a, b, preferred_element_type=F32)


def _dot_nt(a, b):
    return lax.dot_general(a, b, (((1,), (1,)), ((), ())), preferred_element_type=F32)


N_PROJ_OUTS = 11


def _proj_kernel(x_ref, ng_ref, w_ref, qg_ref, kg_ref, gq_ref, gk_ref, cos_ref, sin_ref, *refs):
    (qT_ref, kn_ref, v_ref, ag_ref, iqT_ref, ikw_ref, ikwT_ref,
     rq_ref, rk_ref, rv_ref, rg_ref) = refs[-N_PROJ_OUTS:]
    x = x_ref[...]
    ms = jnp.mean(x * x, axis=-1, keepdims=True)
    xn = ((x * lax.rsqrt(ms + RMS_EPS)) * ng_ref[...]).astype(BF16)

    def proj(c0, c1):
        return _dot(xn, w_ref[:, c0:c1])

    def head_ms(u, gmat_ref):
        sq = u * u
        hi = sq.astype(BF16)
        lo = (sq - hi.astype(F32)).astype(BF16)
        return _dot(hi, gmat_ref[...]) + _dot(lo, gmat_ref[...])

    uq = proj(C_Q, C_K)
    qn = (uq * lax.rsqrt(head_ms(uq, gq_ref) + RMS_EPS)) * qg_ref[...]
    qT_ref[...] = (qn * (ATTN_SCALE * LOG2_E)).T.astype(BF16)

    uk = proj(C_K, C_V)
    kn_ref[...] = (uk * lax.rsqrt(head_ms(uk, gk_ref) + RMS_EPS)) * kg_ref[...]
    v_ref[...] = proj(C_V, C_AG)
    ag_ref[...] = proj(C_AG, C_IQ)
    iqT_ref[...] = proj(C_IQ, C_RQ).T.astype(BF16)
    ikw = proj(C_IKW, W_COLS)
    ikw_ref[...] = ikw
    ikwT_ref[...] = ikw.T

    cosf = cos_ref[...]
    sinf = sin_ref[...]

    def rotary(u):
        parts = []
        for h in range(RET_HEADS):
            xh = u[:, h * RET_DK:(h + 1) * RET_DK]
            parts.append(xh * cosf + pltpu.roll(xh, RET_DK // 2, 1) * sinf)
        return jnp.concatenate(parts, axis=1)

    rq_ref[...] = rotary(proj(C_RQ, C_RK)).astype(BF16)
    rk_ref[...] = (rotary(proj(C_RK, C_RV)) * (RET_DK ** -0.5)).astype(BF16)
    rv_ref[...] = proj(C_RV, C_RG).astype(BF16)
    rg_ref[...] = proj(C_RG, C_IKW)


def _proj_call(x, ng, w, qg, kg, gq, gk, cosf, sinf, tm, key_bufs, buf_shape, layer, n_past):
    B, T, _ = x.shape
    nt = T // tm
    assert n_past % tm == 0
    row = lambda b, i: (b, i, 0)
    col = lambda b, i: (b, 0, i)
    const = lambda b, i: (0, 0)
    tok = lambda width: pl.BlockSpec((None, tm, width), row)
    tokT = lambda width: pl.BlockSpec((None, width, tm), col)
    buf = pl.BlockSpec((None, None, tm, LANES), lambda b, i: (layer, b, n_past // tm + i, 0))
    buf_sds = jax.ShapeDtypeStruct(buf_shape, F32)
    out_shape = (
        jax.ShapeDtypeStruct((B, ATTN_WIDTH, T), BF16),
        buf_sds,
        buf_sds,
        jax.ShapeDtypeStruct((B, T, ATTN_WIDTH), F32),
        jax.ShapeDtypeStruct((B, ATTN_WIDTH, T), BF16),
        buf_sds,
        jax.ShapeDtypeStruct((B, LANES, T), F32),
        jax.ShapeDtypeStruct((B, T, RET_WIDTH), BF16),
        jax.ShapeDtypeStruct((B, T, RET_WIDTH), BF16),
        jax.ShapeDtypeStruct((B, T, RET_WIDTH), BF16),
        jax.ShapeDtypeStruct((B, T, RET_WIDTH), F32),
    )
    out_specs = (tokT(ATTN_WIDTH), buf, buf, tok(ATTN_WIDTH), tokT(ATTN_WIDTH),
                 buf, tokT(LANES), tok(RET_WIDTH), tok(RET_WIDTH), tok(RET_WIDTH), tok(RET_WIDTH))
    n_in = 9
    extra_in, extra_specs, aliases = (), [], {}
    if key_bufs is not None:
        extra_in = tuple(key_bufs)
        extra_specs = [pl.BlockSpec(memory_space=pl.ANY)] * 3
        aliases = {n_in: 1, n_in + 1: 2, n_in + 2: 5}
    in_specs = [
        tok(D_MODEL),
        pl.BlockSpec((1, D_MODEL), const),
        pl.BlockSpec((D_MODEL, W_COLS), const),
        pl.BlockSpec((1, ATTN_WIDTH), const),
        pl.BlockSpec((1, KV_WIDTH), const),
        pl.BlockSpec((ATTN_WIDTH, ATTN_WIDTH), const),
        pl.BlockSpec((KV_WIDTH, KV_WIDTH), const),
        pl.BlockSpec((tm, LANES), lambda b, i: (i, 0)),
        pl.BlockSpec((tm, LANES), lambda b, i: (i, 0)),
    ]
    return pl.pallas_call(
        _proj_kernel, grid=(B, nt), in_specs=in_specs + extra_specs, out_specs=out_specs,
        out_shape=out_shape, input_output_aliases=aliases,
        compiler_params=pltpu.CompilerParams(dimension_semantics=("parallel", "parallel"),
                                             vmem_limit_bytes=VMEM_LIMIT),
        name="proj",
    )(x, ng, w, qg, kg, gq, gk, cosf, sinf, *extra_in)


def _loop_pairs(n, body, init):
    def quad(i, carry):
        for r in range(4):
            carry = body(4 * i + r, carry)
        return carry

    def pair(i, carry):
        return body(2 * i + 1, body(2 * i, carry))

    carry = lax.fori_loop(0, n >> 2, quad, init)
    carry = lax.fori_loop((n >> 2) << 1, n >> 1, pair, carry)
    return lax.fori_loop(n & ~1, n, body, carry)


def _attn_kernel(iqT_ref, qT_ref, ikwT_ref, ikw_ref, kn_ref, v_ref, tri_ref, out_ref,
                 ikw16_ref, kn16_ref, vaug_ref, sc_ref, hi_ref, lo_ref, bias_ref, lg_ref, acc_ref,
                 *, pos0, ksel, nkb_max, lq):
    q0 = pos0 + pl.program_id(1) * lq
    lane = lax.broadcasted_iota(jnp.int32, (1, lq), 1)
    limit = (((q0 + lane) >> 6) + 1) << 6
    last_limit = (((q0 + lq - 1) >> 6) + 1) << 6
    nkb = jnp.minimum((last_limit + KEY_BLOCK - 1) // KEY_BLOCK, nkb_max)
    sub = lax.broadcasted_iota(jnp.int32, (KEY_BLOCK, 1), 0)

    @pl.when(pl.program_id(1) == 0)
    def _():
        ones = jnp.ones((DEN_ROWS, KEY_BLOCK), F32)

        def stage_body(kb, carry):
            off = pl.multiple_of(kb * KEY_BLOCK, KEY_BLOCK)
            ikw16_ref[pl.ds(off, KEY_BLOCK), :] = ikw_ref[pl.ds(off, KEY_BLOCK), :].astype(BF16)
            kn16_ref[pl.ds(off, KEY_BLOCK), :] = kn_ref[pl.ds(off, KEY_BLOCK), :].astype(BF16)
            vT = v_ref[pl.ds(off, KEY_BLOCK), :].T
            for g in range(N_KV_HEADS):
                vaug_ref[g, :, pl.ds(off, KEY_BLOCK)] = jnp.concatenate(
                    [vT[g * HEAD_DIM:(g + 1) * HEAD_DIM, :], ones], axis=0).astype(BF16)
            return carry

        lax.fori_loop(0, nkb_max, stage_body, 0)

    iqT = iqT_ref[...]
    qT = qT_ref[...]
    zpad = jnp.zeros((HEAD_DIM, lq), BF16)
    rhs_idx = [jnp.concatenate([iqT[h * IDX_DIM:(h + 1) * IDX_DIM, :], zpad], axis=0)
               for h in range(IDX_HEADS)]
    wT = ikwT_ref[...]
    wrows = [wT[IDX_DIM + h:IDX_DIM + h + 1, :] * (IDX_SCALE * IDX_HEAD_SCALE) for h in range(IDX_HEADS)]

    def score_body(kb, carry):
        off = pl.multiple_of(kb * KEY_BLOCK, KEY_BLOCK)
        kk = ikw16_ref[pl.ds(off, KEY_BLOCK), :]
        acc = jnp.maximum(_dot(kk, rhs_idx[0]), 0.0) * wrows[0]
        for h in range(1, IDX_HEADS):
            acc = acc + jnp.maximum(_dot(kk, rhs_idx[h]), 0.0) * wrows[h]
        bits = pltpu.bitcast(acc, jnp.int32)
        key = bits ^ ((bits >> 31) & 0x7FFFFFFF)
        key = jnp.where(key == -1, 0, key)
        key = jnp.where(off + sub < limit, key, INT_MIN)
        sc_ref[pl.ds(off, KEY_BLOCK), :] = key
        hi_ref[pl.ds(off, KEY_BLOCK), :] = (key >> 16).astype(jnp.int16)
        lo_ref[pl.ds(off, KEY_BLOCK), :] = ((key & 0xFFFF) + I16_MIN).astype(jnp.int16)
        return carry

    _loop_pairs(nkb, score_body, 0)

    kf = float(ksel)

    def count16(ref, pred_fn):
        def body(kb, c):
            off = pl.multiple_of(kb * KEY_BLOCK, KEY_BLOCK)
            hit = jnp.where(pred_fn(ref[pl.ds(off, KEY_BLOCK), :]), jnp.int16(1), jnp.int16(0))
            parts = [hit[r:r + I16_ROWS, :] for r in range(0, KEY_BLOCK, I16_ROWS)]
            while len(parts) > 1:
                parts = [parts[r] + parts[r + 1] for r in range(0, len(parts), 2)]
            return c + parts[0]
        part = _loop_pairs(nkb, body, jnp.zeros((I16_ROWS, lq), jnp.int16))
        return jnp.sum(part.astype(F32), axis=0, keepdims=True)

    def kth_largest16(ref, k_row):
        def bit_body(i, t):
            cand = t + jnp.left_shift(jnp.int32(1), 15 - i)
            cand16 = cand.astype(jnp.int16)
            c = count16(ref, lambda blk: blk >= cand16)
            return jnp.where(c >= k_row, cand, t)
        return lax.fori_loop(0, 16, bit_body, jnp.full((1, lq), I16_MIN, jnp.int32))

    t_hi = kth_largest16(hi_ref, kf)
    t_hi16 = t_hi.astype(jnp.int16)
    k_lo = kf - count16(hi_ref, lambda blk: blk > t_hi16)

    def plane_body(kb, carry):
        off = pl.multiple_of(kb * KEY_BLOCK, KEY_BLOCK)
        lo_ref[pl.ds(off, KEY_BLOCK), :] = jnp.where(
            hi_ref[pl.ds(off, KEY_BLOCK), :] == t_hi16, lo_ref[pl.ds(off, KEY_BLOCK), :], jnp.int16(I16_MIN))
        return carry

    lax.fori_loop(0, nkb, plane_body, 0)
    t_lo = kth_largest16(lo_ref, k_lo)
    t_lo16 = t_lo.astype(jnp.int16)
    thr = (t_hi << 16) | (t_lo - I16_MIN)
    c_gt_lo = count16(lo_ref, lambda blk: blk > t_lo16)
    need = jnp.where(thr == INT_MIN, 0.0, k_lo - c_gt_lo)

    def qpad(h):
        qh = qT[h * HEAD_DIM:(h + 1) * HEAD_DIM, :]
        return jnp.concatenate([qh, zpad] if h // KV_GROUP == 0 else [zpad, qh], axis=0)

    rhs_q = [qpad(h) for h in range(N_HEADS)]

    acc_ref[...] = jnp.zeros_like(acc_ref)
    max_init = tuple(jnp.full((SUBLANES, lq), NEG_INF, F32) for _ in range(KV_GROUP))

    def logits_group(g, off, bias, maxes):
        kb16 = kn16_ref[pl.ds(off, KEY_BLOCK), :]
        new = []
        for hl in range(KV_GROUP):
            h = g * KV_GROUP + hl
            lg = _dot(kb16, rhs_q[h]) + bias
            lg_ref[h, pl.ds(off, KEY_BLOCK), :] = lg
            new.append(jnp.maximum(maxes[hl], jnp.max(lg.reshape(KEY_BLOCK // SUBLANES, SUBLANES, lq), axis=0)))
        return tuple(new)

    def pv_group(g, off, ms):
        vaug = vaug_ref[g, :, pl.ds(off, KEY_BLOCK)]
        for hl in range(KV_GROUP):
            h = g * KV_GROUP + hl
            p = jnp.exp2(lg_ref[h, pl.ds(off, KEY_BLOCK), :] - ms[hl]).astype(BF16)
            acc_ref[h] += _dot(vaug, p)

    def body_a(kb, carry):
        run_eq = carry[0]
        off = pl.multiple_of(kb * KEY_BLOCK, KEY_BLOCK)
        blk = sc_ref[pl.ds(off, KEY_BLOCK), :]
        eqf = jnp.where(blk == thr, 1.0, 0.0)
        rank = run_eq + _dot(tri_ref[...], eqf.astype(BF16))
        sel = jnp.where(blk > thr, 1.0, jnp.where(rank < need, eqf, 0.0))
        bias = (1.0 - sel) * NEG_INF
        bias_ref[pl.ds(off, KEY_BLOCK), :] = bias
        run_eq = run_eq + jnp.sum(eqf, axis=0, keepdims=True)
        return (run_eq,) + logits_group(0, off, bias, carry[1:])

    carry = _loop_pairs(nkb, body_a, (jnp.zeros((1, lq), F32),) + max_init)
    ms0 = [jnp.max(c, axis=0, keepdims=True) for c in carry[1:]]

    def body_b(kb, maxes):
        off = pl.multiple_of(kb * KEY_BLOCK, KEY_BLOCK)
        new = logits_group(1, off, bias_ref[pl.ds(off, KEY_BLOCK), :], maxes)
        pv_group(0, off, ms0)
        return new

    ms1 = [jnp.max(c, axis=0, keepdims=True) for c in _loop_pairs(nkb, body_b, max_init)]

    def body_c(kb, carry):
        pv_group(1, pl.multiple_of(kb * KEY_BLOCK, KEY_BLOCK), ms1)
        return carry

    _loop_pairs(nkb, body_c, 0)

    outs = []
    for h in range(N_HEADS):
        a = acc_ref[h]
        outs.append(a[0:HEAD_DIM, :] * (1.0 / a[HEAD_DIM:HEAD_DIM + 1, :]))
    out_ref[...] = jnp.concatenate(outs, axis=0).T


def _attn_call(iqT, qT, ikwT, ikw_keys, kn_keys, v_keys, tri, layer, pos0, t_real, ksel, lq):
    B, _, T = qT.shape
    Lk = ikw_keys.shape[2]
    qblk = lambda rows: pl.BlockSpec((None, rows, lq), lambda b, j: (b, 0, j))
    keys = pl.BlockSpec((None, None, Lk, LANES), lambda b, j: (layer, b, 0, 0))
    kern = functools.partial(_attn_kernel, pos0=pos0, ksel=ksel, nkb_max=Lk // KEY_BLOCK, lq=lq)
    return pl.pallas_call(
        kern, grid=(B, -(-t_real // lq)),
        in_specs=[qblk(ATTN_WIDTH), qblk(ATTN_WIDTH), qblk(LANES), keys, keys, keys,
                  pl.BlockSpec((KEY_BLOCK, KEY_BLOCK), lambda b, j: (0, 0))],
        out_specs=pl.BlockSpec((None, lq, ATTN_WIDTH), lambda b, j: (b, j, 0)),
        out_shape=jax.ShapeDtypeStruct((B, T, ATTN_WIDTH), F32),
        scratch_shapes=[pltpu.VMEM((Lk, LANES), BF16),
                        pltpu.VMEM((Lk, LANES), BF16),
                        pltpu.VMEM((N_KV_HEADS, HEAD_DIM + DEN_ROWS, Lk), BF16),
                        pltpu.VMEM((Lk, lq), jnp.int32),
                        pltpu.VMEM((Lk, lq), jnp.int16),
                        pltpu.VMEM((Lk, lq), jnp.int16),
                        pltpu.VMEM((Lk, lq), F32),
                        pltpu.VMEM((N_HEADS, Lk, lq), F32),
                        pltpu.VMEM((N_HEADS, HEAD_DIM + DEN_ROWS, lq), F32)],
        compiler_params=pltpu.CompilerParams(dimension_semantics=("arbitrary", "arbitrary"),
                                             vmem_limit_bytes=VMEM_LIMIT),
        name="attn",
    )(iqT, qT, ikwT, ikw_keys, kn_keys, v_keys, tri)


def _ret_kernel(rq_ref, rk_ref, rv_ref, s0_ref, dmask_ref, zeta_ref, xi_ref, gch_ref, g_ref,
                o_ref, sfin_ref, st_ref, *, n_last):
    n = pl.program_id(1)

    @pl.when(n == 0)
    def _():
        st_ref[...] = s0_ref[...]

    for b in range(rq_ref.shape[0]):
        for h in range(RET_HEADS):
            sl = slice(h * RET_DK, (h + 1) * RET_DK)
            q = rq_ref[b, :, sl]
            k = rk_ref[b, :, sl]
            v = rv_ref[b, :, sl]
            inner = _dot_nt(q, k) * dmask_ref[h]
            o = _dot(inner.astype(BF16), v)
            r = st_ref[b, h]
            o = o + _dot(q, r.astype(BF16)) * xi_ref[h]
            kz = k.astype(F32) * zeta_ref[h]
            u = _dot(kz.T.astype(BF16), v)
            st_ref[b, h] = gch_ref[h] * r + u
            ms = jnp.mean(o * o, axis=-1, keepdims=True)
            o_ref[b, :, sl] = (o * lax.rsqrt(ms + RMS_EPS)) * g_ref[:, sl]

    @pl.when(n == n_last)
    def _():
        sfin_ref[...] = st_ref[...]


def _ret_tables(c):
    log_g = jnp.log(1.0 - 2.0 ** (-5.0 - jnp.arange(RET_HEADS, dtype=F32)))
    i = jnp.arange(c, dtype=F32)
    diff = i[:, None] - i[None, :]
    dmask = jnp.where(diff[None] >= 0, jnp.exp(log_g[:, None, None] * jnp.maximum(diff, 0.0)[None]), 0.0)
    zeta = jnp.exp(log_g[:, None] * (c - 1 - i)[None, :])
    xi = jnp.exp(log_g[:, None] * (i + 1)[None, :])
    gch = jnp.exp(log_g * c)
    bc = lambda t: jnp.broadcast_to(t[:, :, None], (RET_HEADS, c, LANES))
    return dmask, bc(zeta), bc(xi), jnp.broadcast_to(gch[:, None, None], (RET_HEADS, 1, LANES))


def _ret_call(rq, rk, rv, s0, ret_g, t_real, c):
    B, T, _ = rq.shape
    nc = T // c
    dmask, zeta, xi, gch = _ret_tables(c)
    bb = RET_BATCH_ROWS if B % RET_BATCH_ROWS == 0 else 1
    tok =pl.BlockSpec((bb, c, RET_WIDTH), lambda b, n: (b, n, 0))
    st = pl.BlockSpec((bb, RET_HEADS, RET_DK, RET_DV), lambda b, n: (b, 0, 0, 0))
    tab = lambda s: pl.BlockSpec(s, lambda b, n: (0,) * len(s))
    return pl.pallas_call(
        functools.partial(_ret_kernel, n_last=t_real // c - 1), grid=(B // bb, nc),
        in_specs=[tok, tok, tok, st, tab((RET_HEADS, c, c)), tab((RET_HEADS, c, LANES)),
                  tab((RET_HEADS, c, LANES)), tab((RET_HEADS, 1, LANES)), tab((1, RET_WIDTH))],
        out_specs=(tok, st),
        out_shape=(jax.ShapeDtypeStruct((B, T, RET_WIDTH), F32),
                   jax.ShapeDtypeStruct((B, RET_HEADS, RET_DK, RET_DV), F32)),
        scratch_shapes=[pltpu.VMEM((bb, RET_HEADS, RET_DK, RET_DV), F32)],
        compiler_params=pltpu.CompilerParams(dimension_semantics=("parallel", "arbitrary"),
                                             vmem_limit_bytes=VMEM_LIMIT),
        name="ret",
    )(rq, rk, rv, s0, dmask, zeta, xi, gch, ret_g)


def _out_kernel(h_ref, attn_ref, ag_ref, ret_ref, rg_ref, p_ref, wo_ref, pg_ref, pp_ref, png_ref, o_ref):
    ag = ag_ref[...]
    rg = rg_ref[...]
    mix_a = (attn_ref[...] * (ag * jax.nn.sigmoid(ag))).astype(BF16)
    mix_r = (ret_ref[...] * (rg * jax.nn.sigmoid(rg))).astype(BF16)
    h1 = h_ref[...] + _dot(mix_a, wo_ref[0:ATTN_WIDTH, :]) + _dot(mix_r, wo_ref[ATTN_WIDTH:, :])
    ms = jnp.mean(h1 * h1, axis=-1, keepdims=True)
    hn = ((h1 * lax.rsqrt(ms + RMS_EPS)) * png_ref[...]).astype(BF16)
    gate = jax.nn.sigmoid(_dot(hn, pg_ref[...]))
    o_ref[...] = h1 + gate * _dot(p_ref[...].astype(BF16), pp_ref[...])


def _out_call(h, attn, ag, ret, rg, p, wo, pgate, pproj, png, tm):
    B, T, _ = h.shape
    tok = lambda width: pl.BlockSpec((None, tm, width), lambda b, i: (b, i, 0))
    const = lambda s: pl.BlockSpec(s, lambda b, i: (0, 0))
    return pl.pallas_call(
        _out_kernel, grid=(B, T // tm),
        in_specs=[tok(D_MODEL), tok(ATTN_WIDTH), tok(ATTN_WIDTH), tok(RET_WIDTH), tok(RET_WIDTH), tok(PLE_DIM),
                  const((D_MODEL, D_MODEL)), const((D_MODEL, D_MODEL)), const((PLE_DIM, D_MODEL)),
                  const((1, D_MODEL))],
        out_specs=tok(D_MODEL),
        out_shape=jax.ShapeDtypeStruct((B, T, D_MODEL), F32),
        compiler_params=pltpu.CompilerParams(dimension_semantics=("parallel", "parallel"),
                                             vmem_limit_bytes=VMEM_LIMIT),
        name="out",
    )(h, attn, ag, ret, rg, p, wo, pgate, pproj, png)


def _rope_tables(pos0, t):
    half = RET_DK // 2
    inv = ROPE_BASE ** (-jnp.arange(half, dtype=F32) / half)
    ang = (pos0 + jnp.arange(t, dtype=jnp.int32)).astype(F32)[:, None] * inv[None, :]
    cos, sin = jnp.cos(ang), jnp.sin(ang)
    return jnp.concatenate([cos, cos], axis=1), jnp.concatenate([-sin, sin], axis=1)


def _group_mean_matrix(width, group):
    i = jnp.arange(width) // group
    return jnp.where(i[:, None] == i[None, :], 1.0 / group, 0.0).astype(BF16)


def _strict_lower(n):
    return (jnp.arange(n)[None, :] < jnp.arange(n)[:, None]).astype(BF16)


def _key_rows(n_past, t):
    return -(-(n_past + t) // KEY_BLOCK) * KEY_BLOCK


def _layer(h, p, layer, depth, n_past, t_real, key_bufs, s0, lw, consts, tm, chunk, lq):
    w, wo, ng, qg, kg, ret_g, pproj, pgate, png = lw
    gq, gk, tri = consts
    B, T, _ = h.shape
    cosf, sinf = _rope_tables(n_past, T)
    buf_shape = (depth, B, _key_rows(n_past, T), LANES)
    qT, kn_buf, v_buf, ag, iqT, ikw_buf, ikwT, rq, rk, rv, rg = _proj_call(
        h, ng, w, qg, kg, gq, gk, cosf, sinf, tm, key_bufs, buf_shape, layer, n_past)
    ksel = min(TOPK_MAX, (n_past + t_real) // 4)
    attn = _attn_call(iqT, qT, ikwT, ikw_buf, kn_buf, v_buf, tri, layer, n_past, t_real, ksel, lq)
    ret, s_new = _ret_call(rq, rk, rv, s0, ret_g, t_real, chunk)
    h_new = _out_call(h, attn, ag, ret, rg, p, wo, pgate, pproj, png, tm)
    return h_new, (kn_buf, v_buf, ikw_buf), s_new


def _permute_w_in(w_in):
    depth = w_in.shape[0]
    a = w_in[:, :, 0:1280]
    iq = w_in[:, :, 1280:1792]
    ik_iw = w_in[:, :, 1792:1864]
    r = w_in[:, :, 1864:3912]
    pad = jnp.zeros((depth, D_MODEL, W_COLS - C_IKW - 72), w_in.dtype)
    return jnp.concatenate([a, iq, r, ik_iw, pad], axis=2).astype(BF16)


def kernel(x_prompt, x_sample, cache_k, cache_v, cache_k_idx, state_ret, p_prompt, p_sample,
           w_in, w_out, norm_g, q_norm_g, k_norm_g, ret_norm_g, ple_proj, ple_gate, ple_norm_g):
    depth = w_in.shape[0]
    b_p, t_p, _ = x_prompt.shape
    b_s, t_s, _ = x_sample.shape
    past = cache_k.shape[2]
    t_s_pad = -(-t_s // LANES) * LANES

    w_perm = _permute_w_in(w_in)
    wo = w_out.astype(BF16)
    pproj = ple_proj.astype(BF16)
    pgate = ple_gate.astype(BF16)
    qg = jnp.tile(q_norm_g, (1, N_HEADS))[:, None, :]
    kg = jnp.tile(k_norm_g, (1, N_KV_HEADS))[:, None, :]
    consts = (_group_mean_matrix(ATTN_WIDTH, HEAD_DIM), _group_mean_matrix(KV_WIDTH, HEAD_DIM),
              _strict_lower(KEY_BLOCK))

    pad_t = lambda a, axis: jnp.pad(a, [(0, t_s_pad - t_s) if d == axis else (0, 0) for d in range(a.ndim)])
    h_p = x_prompt
    h_s = pad_t(x_sample, 1)
    p_s = pad_t(p_sample, 2)
    r0 = jnp.zeros((b_p, RET_HEADS, RET_DK, RET_DV), F32)
    grow = lambda a: jnp.pad(a, ((0, 0), (0, 0), (0, _key_rows(past, t_s_pad) - past), (0, LANES - a.shape[-1])))
    bufs_s = (grow(cache_k.reshape(depth, b_s, past, KV_WIDTH)), grow(cache_v.reshape(depth, b_s, past, KV_WIDTH)),
              grow(cache_k_idx))
    bufs_p = None

    states_p, states_s = [], []
    for i in range(depth):
        lw = (w_perm[i], wo[i], norm_g[i][None], qg[i], kg[i], ret_norm_g[i][None], pproj[i], pgate[i],
              ple_norm_g[i][None])
        h_p, bufs_p, rp = _layer(h_p, p_prompt[i], i, depth, 0, t_p, bufs_p, r0, lw, consts,
                                 PROMPT_ROWS, 2 * CHUNK, 2 * LANES)
        h_s, bufs_s, rs = _layer(h_s, p_s[i], i, depth, past, t_s, bufs_s, state_ret[i], lw, consts,
                                 LANES, CHUNK, LANES)
        states_p.append(rp)
        states_s.append(rs)

    heads = lambda a: a.reshape(a.shape[:-1] + (N_KV_HEADS, HEAD_DIM))
    new_s = [b[:, :, past:past + t_s] for b in bufs_s]
    return (h_p, h_s[:, :t_s],
            heads(bufs_p[0]), heads(bufs_p[1]), bufs_p[2][..., :IDX_DIM], jnp.stack(states_p),
            heads(new_s[0]), heads(new_s[1]), new_s[2][..., :IDX_DIM], jnp.stack(states_s))
```

```python
import functools

import jax
import jax.numpy as jnp
from jax import lax
from jax.experimental import pallas as pl
from jax.experimental.pallas import tpu as pltpu

F32 = jnp.float32
BF16 = jnp.bfloat16

D_MODEL = 1024
CHUNK = 64
ATTN_WIDTH = 512
RET_WIDTH = 512
N_HEADS = 8
HEAD_DIM = 64
N_KV_HEADS = 2
KV_GROUP = N_HEADS // N_KV_HEADS
KV_WIDTH = N_KV_HEADS * HEAD_DIM
IDX_HEADS = 8
IDX_DIM = 64
TOPK_MAX = 256
RET_HEADS = 4
RET_DK = 128
RET_DV = 128
PLE_DIM = 256
RMS_EPS = 1e-6
NEG_INF = -1e30
ATTN_SCALE = HEAD_DIM ** -0.5
IDX_SCALE = IDX_DIM ** -0.5
IDX_HEAD_SCALE = IDX_HEADS ** -0.5
ROPE_BASE = 10000.0
LOG2_E = 1.4426950408889634

LANES = 128
SUBLANES = 8
MXU_WIDTH = 256
KEY_BLOCK = MXU_WIDTH
PROMPT_ROWS = 512
RET_BATCH_ROWS = 4
DEN_ROWS = 16
INT_MIN = -(2 ** 31)
I16_MIN = -(2 ** 15)
I16_ROWS = 16
VMEM_LIMIT = 56 * 1024 * 1024

C_Q = 0
C_K = 512
C_V = 640
C_AG = 768
C_IQ = 1280
C_RQ = 1792
C_RK = 2304
C_RV = 2816
C_RG = 3328
C_IKW = 3840
W_COLS = 3968


def _dot(a, b):
    return jnp.dot(a, b, preferred_element_type=F32)


def _dot_nt(a, b):
    return lax.dot_general(a, b, (((1,), (1,)), ((), ())), preferred_element_type=F32)


N_PROJ_OUTS = 12


def _proj_kernel(x_ref, ng_ref, w_ref, qg_ref, kg_ref, gq_ref, gk_ref, cos_ref, sin_ref, *refs):
    (qT_ref, kn_ref, v_ref, ag_ref, iqT_ref, ikw_ref, ikwT_ref,
     rq_ref, rk_ref, rv_ref, rg_ref, kidx_ref) = refs[-N_PROJ_OUTS:]
    x = x_ref[...]
    ms = jnp.mean(x * x, axis=-1, keepdims=True)
    xn = ((x * lax.rsqrt(ms + RMS_EPS)) * ng_ref[...]).astype(BF16)

    def proj(c0, c1):
        return _dot(xn, w_ref[:, c0:c1])

    def head_ms(u, gmat_ref):
        sq = u * u
        hi = sq.astype(BF16)
        lo = (sq - hi.astype(F32)).astype(BF16)
        return _dot(hi, gmat_ref[...]) + _dot(lo, gmat_ref[...])

    uq = proj(C_Q, C_K)
    qn = (uq * lax.rsqrt(head_ms(uq, gq_ref) + RMS_EPS)) * qg_ref[...]
    qT_ref[...] = (qn * (ATTN_SCALE * LOG2_E)).T.astype(BF16)

    uk = proj(C_K, C_V)
    kn_ref[...] = (uk * lax.rsqrt(head_ms(uk, gk_ref) + RMS_EPS)) * kg_ref[...]
    v_ref[...] = proj(C_V, C_AG)
    ag_ref[...] = proj(C_AG, C_IQ)
    iqT_ref[...] = proj(C_IQ, C_RQ).T.astype(BF16)
    ikw = proj(C_IKW, W_COLS)
    ikw_ref[...] = ikw
    kidx_ref[...] = ikw[:, 0:IDX_DIM]
    ikwT_ref[...] = ikw.T

    cosf = cos_ref[...]
    sinf = sin_ref[...]

    def rotary(u):
        parts = []
        for h in range(RET_HEADS):
            xh = u[:, h * RET_DK:(h + 1) * RET_DK]
            parts.append(xh * cosf + pltpu.roll(xh, RET_DK // 2, 1) * sinf)
        return jnp.concatenate(parts, axis=1)

    rq_ref[...] = rotary(proj(C_RQ, C_RK)).astype(BF16)
    rk_ref[...] = (rotary(proj(C_RK, C_RV)) * (RET_DK ** -0.5)).astype(BF16)
    rv_ref[...] = proj(C_RV, C_RG).astype(BF16)
    rg_ref[...] = proj(C_RG, C_IKW)


def _proj_call(x, ng, w, qg, kg, gq, gk, cosf, sinf, tm, key_bufs, buf_shape, layer, n_past):
    B, T, _ = x.shape
    nt = T // tm
    assert n_past % tm == 0
    row = lambda b, i: (b, i, 0)
    col = lambda b, i: (b, 0, i)
    const = lambda b, i: (0, 0)
    tok = lambda width: pl.BlockSpec((None, tm, width), row)
    tokT = lambda width: pl.BlockSpec((None, width, tm), col)
    buf = pl.BlockSpec((None, None, tm, LANES), lambda b, i: (layer, b, n_past // tm + i, 0))
    buf_sds = jax.ShapeDtypeStruct(buf_shape, F32)
    out_shape = (
        jax.ShapeDtypeStruct((B, ATTN_WIDTH, T), BF16),
        buf_sds,
        buf_sds,
        jax.ShapeDtypeStruct((B, T, ATTN_WIDTH), F32),
        jax.ShapeDtypeStruct((B, ATTN_WIDTH, T), BF16),
        buf_sds,
        jax.ShapeDtypeStruct((B, LANES, T), F32),
        jax.ShapeDtypeStruct((B, T, RET_WIDTH), BF16),
        jax.ShapeDtypeStruct((B, T, RET_WIDTH), BF16),
        jax.ShapeDtypeStruct((B, T, RET_WIDTH), BF16),
        jax.ShapeDtypeStruct((B, T, RET_WIDTH), F32),
        jax.ShapeDtypeStruct((buf_shape[0], B, T, IDX_DIM), F32),
    )
    kidx = pl.BlockSpec((None, None, tm, IDX_DIM), lambda b, i: (layer, b, i, 0))
    out_specs = (tokT(ATTN_WIDTH), buf, buf, tok(ATTN_WIDTH), tokT(ATTN_WIDTH),
                 buf, tokT(LANES), tok(RET_WIDTH), tok(RET_WIDTH), tok(RET_WIDTH), tok(RET_WIDTH), kidx)
    n_in = 9
    extra_in, extra_specs, aliases = (), [], {}
    if key_bufs is not None:
        extra_in = tuple(key_bufs)
        extra_specs = [pl.BlockSpec(memory_space=pl.ANY)] * 4
        aliases = {n_in: 1, n_in + 1: 2, n_in + 2: 5, n_in + 3: 11}
    in_specs = [
        tok(D_MODEL),
        pl.BlockSpec((1, D_MODEL), const),
        pl.BlockSpec((D_MODEL, W_COLS), const),
        pl.BlockSpec((1, ATTN_WIDTH), const),
        pl.BlockSpec((1, KV_WIDTH), const),
        pl.BlockSpec((ATTN_WIDTH, ATTN_WIDTH), const),
        pl.BlockSpec((KV_WIDTH, KV_WIDTH), const),
        pl.BlockSpec((tm, LANES), lambda b, i: (i, 0)),
        pl.BlockSpec((tm, LANES), lambda b, i: (i, 0)),
    ]
    return pl.pallas_call(
        _proj_kernel, grid=(B, nt), in_specs=in_specs + extra_specs, out_specs=out_specs,
        out_shape=out_shape, input_output_aliases=aliases,
        compiler_params=pltpu.CompilerParams(dimension_semantics=("parallel", "parallel"),
                                             vmem_limit_bytes=VMEM_LIMIT),
        name="proj",
    )(x, ng, w, qg, kg, gq, gk, cosf, sinf, *extra_in)


def _loop_pairs(n, body, init):
    def quad(i, carry):
        for r in range(4):
            carry = body(4 * i + r, carry)
        return carry

    def pair(i, carry):
        return body(2 * i + 1, body(2 * i, carry))

    carry = lax.fori_loop(0, n >> 2, quad, init)
    carry = lax.fori_loop((n >> 2) << 1, n >> 1, pair, carry)
    return lax.fori_loop(n & ~1, n, body, carry)


def _attn_kernel(iqT_ref, qT_ref, ikwT_ref, ikw_ref, kn_ref, v_ref, tri_ref, out_ref,
                 ikw16_ref, kn16_ref, vaug_ref, sc_ref, hi_ref, lo_ref, bias_ref, lg_ref, acc_ref,
                 *, pos0, ksel, nkb_max, lq):
    q0 = pos0 + pl.program_id(1) * lq
    lane = lax.broadcasted_iota(jnp.int32, (1, lq), 1)
    limit = (((q0 + lane) >> 6) + 1) << 6
    last_limit = (((q0 + lq - 1) >> 6) + 1) << 6
    nkb = jnp.minimum((last_limit + KEY_BLOCK - 1) // KEY_BLOCK, nkb_max)
    sub = lax.broadcasted_iota(jnp.int32, (KEY_BLOCK, 1), 0)

    @pl.when(pl.program_id(1) == 0)
    def _():
        ones = jnp.ones((DEN_ROWS, KEY_BLOCK), F32)

        def stage_body(kb, carry):
            off = pl.multiple_of(kb * KEY_BLOCK, KEY_BLOCK)
            ikw16_ref[pl.ds(off, KEY_BLOCK), :] = ikw_ref[pl.ds(off, KEY_BLOCK), :].astype(BF16)
            kn16_ref[pl.ds(off, KEY_BLOCK), :] = kn_ref[pl.ds(off, KEY_BLOCK), :].astype(BF16)
            vT = v_ref[pl.ds(off, KEY_BLOCK), :].T
            for g in range(N_KV_HEADS):
                vaug_ref[g, :, pl.ds(off, KEY_BLOCK)] = jnp.concatenate(
                    [vT[g * HEAD_DIM:(g + 1) * HEAD_DIM, :], ones], axis=0).astype(BF16)
            return carry

        lax.fori_loop(0, nkb_max, stage_body, 0)

    iqT = iqT_ref[...]
    qT = qT_ref[...]
    zpad = jnp.zeros((HEAD_DIM, lq), BF16)
    rhs_idx = [jnp.concatenate([iqT[h * IDX_DIM:(h + 1) * IDX_DIM, :], zpad], axis=0)
               for h in range(IDX_HEADS)]
    wT = ikwT_ref[...]
    wrows = [wT[IDX_DIM + h:IDX_DIM + h + 1, :] * (IDX_SCALE * IDX_HEAD_SCALE) for h in range(IDX_HEADS)]

    hp = max(1, MXU_WIDTH // lq)
    side_by_side = lambda ops: [jnp.concatenate(ops[i:i + hp], axis=1) for i in range(0, len(ops), hp)]

    def dot_heads(lhs, rhs_wide):
        res = _dot(lhs, rhs_wide)
        return [res[:, j * lq:(j + 1) * lq] for j in range(hp)]

    rhs_idx = side_by_side(rhs_idx)

    def score_body(kb, carry):
        off = pl.multiple_of(kb * KEY_BLOCK, KEY_BLOCK)
        kk = ikw16_ref[pl.ds(off, KEY_BLOCK), :]
        acc = None
        for i, rhs in enumerate(rhs_idx):
            for j, s in enumerate(dot_heads(kk, rhs)):
                term = jnp.maximum(s, 0.0) * wrows[i * hp + j]
                acc = term if acc is None else acc + term
        bits = pltpu.bitcast(acc, jnp.int32)
        key = bits ^ ((bits >> 31) & 0x7FFFFFFF)
        key = jnp.where(key == -1, 0, key)
        key = jnp.where(off + sub < limit, key, INT_MIN)
        sc_ref[pl.ds(off, KEY_BLOCK), :] = key
        hi_ref[pl.ds(off, KEY_BLOCK), :] = (key >> 16).astype(jnp.int16)
        lo_ref[pl.ds(off, KEY_BLOCK), :] = ((key & 0xFFFF) + I16_MIN).astype(jnp.int16)
        return carry

    _loop_pairs(nkb, score_body, 0)

    kf = float(ksel)

    def count16(ref, pred_fn):
        def body(kb, c):
            off = pl.multiple_of(kb * KEY_BLOCK, KEY_BLOCK)
            hit = jnp.where(pred_fn(ref[pl.ds(off, KEY_BLOCK), :]), jnp.int16(1), jnp.int16(0))
            parts = [hit[r:r + I16_ROWS, :] for r in range(0, KEY_BLOCK, I16_ROWS)]
            while len(parts) > 1:
                parts = [parts[r] + parts[r + 1] for r in range(0, len(parts), 2)]
            return c + parts[0]
        part = _loop_pairs(nkb, body, jnp.zeros((I16_ROWS, lq), jnp.int16))
        return jnp.sum(part.astype(F32), axis=0, keepdims=True)

    def kth_largest16(ref, k_row):
        def bit_body(i, t):
            cand = t + jnp.left_shift(jnp.int32(1), 15 - i)
            cand16 = cand.astype(jnp.int16)
            c = count16(ref, lambda blk: blk >= cand16)
            return jnp.where(c >= k_row, cand, t)
        return lax.fori_loop(0, 16, bit_body, jnp.full((1, lq), I16_MIN, jnp.int32))

    t_hi = kth_largest16(hi_ref, kf)
    t_hi16 = t_hi.astype(jnp.int16)
    k_lo = kf - count16(hi_ref, lambda blk: blk > t_hi16)

    def plane_body(kb, carry):
        off = pl.multiple_of(kb * KEY_BLOCK, KEY_BLOCK)
        lo_ref[pl.ds(off, KEY_BLOCK), :] = jnp.where(
            hi_ref[pl.ds(off, KEY_BLOCK), :] == t_hi16, lo_ref[pl.ds(off, KEY_BLOCK), :], jnp.int16(I16_MIN))
        return carry

    lax.fori_loop(0, nkb, plane_body, 0)
    t_lo = kth_largest16(lo_ref, k_lo)
    t_lo16 = t_lo.astype(jnp.int16)
    thr = (t_hi << 16) | (t_lo - I16_MIN)
    c_gt_lo = count16(lo_ref, lambda blk: blk > t_lo16)
    need = jnp.where(thr == INT_MIN, 0.0, k_lo - c_gt_lo)

    def qpad(h):
        qh = qT[h * HEAD_DIM:(h + 1) * HEAD_DIM, :]
        return jnp.concatenate([qh, zpad] if h // KV_GROUP == 0 else [zpad, qh], axis=0)

    rhs_q = side_by_side([qpad(h) for h in range(N_HEADS)])
    groups_per_kv = KV_GROUP // hp

    acc_ref[...] = jnp.zeros_like(acc_ref)
    max_init = tuple(jnp.full((SUBLANES, lq), NEG_INF, F32) for _ in range(KV_GROUP))

    def logits_group(g, off, bias, maxes):
        kb16 = kn16_ref[pl.ds(off, KEY_BLOCK), :]
        new = []
        for i in range(groups_per_kv):
            for j, raw in enumerate(dot_heads(kb16, rhs_q[g * groups_per_kv + i])):
                hl = i * hp + j
                lg = raw + bias
                lg_ref[g * KV_GROUP + hl, pl.ds(off, KEY_BLOCK), :] = lg
                new.append(jnp.maximum(maxes[hl], jnp.max(lg.reshape(KEY_BLOCK // SUBLANES, SUBLANES, lq), axis=0)))
        return tuple(new)

    def pv_group(g, off, ms):
        vaug = vaug_ref[g, :, pl.ds(off, KEY_BLOCK)]
        for i in range(groups_per_kv):
            hls = range(i * hp, (i + 1) * hp)
            p = [jnp.exp2(lg_ref[g * KV_GROUP + hl, pl.ds(off, KEY_BLOCK), :] - ms[hl]).astype(BF16) for hl in hls]
            for hl, upd in zip(hls, dot_heads(vaug, jnp.concatenate(p, axis=1))):
                acc_ref[g * KV_GROUP + hl] += upd

    def body_a(kb, carry):
        run_eq = carry[0]
        off = pl.multiple_of(kb * KEY_BLOCK, KEY_BLOCK)
        blk = sc_ref[pl.ds(off, KEY_BLOCK), :]
        eqf = jnp.where(blk == thr, 1.0, 0.0)
        rank = run_eq + _dot(tri_ref[...], eqf.astype(BF16))
        sel = jnp.where(blk > thr, 1.0, jnp.where(rank < need, eqf, 0.0))
        bias = (1.0 - sel) * NEG_INF
        bias_ref[pl.ds(off, KEY_BLOCK), :] = bias
        run_eq = run_eq + jnp.sum(eqf, axis=0, keepdims=True)
        return (run_eq,) + logits_group(0, off, bias, carry[1:])

    carry = _loop_pairs(nkb, body_a, (jnp.zeros((1, lq), F32),) + max_init)
    ms0 = [jnp.max(c, axis=0, keepdims=True) for c in carry[1:]]

    def body_b(kb, maxes):
        off = pl.multiple_of(kb * KEY_BLOCK, KEY_BLOCK)
        new = logits_group(1, off, bias_ref[pl.ds(off, KEY_BLOCK), :], maxes)
        pv_group(0, off, ms0)
        return new

    ms1 = [jnp.max(c, axis=0, keepdims=True) for c in _loop_pairs(nkb, body_b, max_init)]

    def body_c(kb, carry):
        pv_group(1, pl.multiple_of(kb * KEY_BLOCK, KEY_BLOCK), ms1)
        return carry

    _loop_pairs(nkb, body_c, 0)

    outs = []
    for h in range(N_HEADS):
        a = acc_ref[h]
        outs.append(a[0:HEAD_DIM, :] * (1.0 / a[HEAD_DIM:HEAD_DIM + 1, :]))
    out_ref[...] = jnp.concatenate(outs, axis=0).T


def _attn_call(iqT, qT, ikwT, ikw_keys, kn_keys, v_keys, tri, layer, pos0, t_real, ksel, lq):
    B, _, T = qT.shape
    Lk = ikw_keys.shape[2]
    qblk = lambda rows: pl.BlockSpec((None, rows, lq), lambda b, j: (b, 0, j))
    keys = pl.BlockSpec((None, None, Lk, LANES), lambda b, j: (layer, b, 0, 0))
    kern = functools.partial(_attn_kernel, pos0=pos0, ksel=ksel, nkb_max=Lk // KEY_BLOCK, lq=lq)
    return pl.pallas_call(
        kern, grid=(B, -(-t_real // lq)),
        in_specs=[qblk(ATTN_WIDTH), qblk(ATTN_WIDTH), qblk(LANES), keys, keys, keys,
                  pl.BlockSpec((KEY_BLOCK, KEY_BLOCK), lambda b, j: (0, 0))],
        out_specs=pl.BlockSpec((None, lq, ATTN_WIDTH), lambda b, j: (b, j, 0)),
        out_shape=jax.ShapeDtypeStruct((B, T, ATTN_WIDTH), F32),
        scratch_shapes=[pltpu.VMEM((Lk, LANES), BF16),
                        pltpu.VMEM((Lk, LANES), BF16),
                        pltpu.VMEM((N_KV_HEADS, HEAD_DIM + DEN_ROWS, Lk), BF16),
                        pltpu.VMEM((Lk, lq), jnp.int32),
                        pltpu.VMEM((Lk, lq), jnp.int16),
                        pltpu.VMEM((Lk, lq), jnp.int16),
                        pltpu.VMEM((Lk, lq), F32),
                        pltpu.VMEM((N_HEADS, Lk, lq), F32),
                        pltpu.VMEM((N_HEADS, HEAD_DIM + DEN_ROWS, lq), F32)],
        compiler_params=pltpu.CompilerParams(dimension_semantics=("arbitrary", "arbitrary"),
                                             vmem_limit_bytes=VMEM_LIMIT),
        name="attn",
    )(iqT, qT, ikwT, ikw_keys, kn_keys, v_keys, tri)


def _ret_kernel(rq_ref, rk_ref, rv_ref, s0_ref, dmask_ref, zeta_ref, xi_ref, gch_ref, g_ref,
                o_ref, sfin_ref, st_ref, *, n_last):
    n = pl.program_id(1)

    @pl.when(n == 0)
    def _():
        st_ref[...] = s0_ref[...]

    pairs = [(b, h) for b in range(rq_ref.shape[0]) for h in range(RET_HEADS)]
    sl = lambda h: slice(h * RET_DK, (h + 1) * RET_DK)
    inner = [(_dot_nt(rq_ref[b, :, sl(h)], rk_ref[b, :, sl(h)]) * dmask_ref[h]).astype(BF16) for b, h in pairs]
    cross = [_dot(rq_ref[b, :, sl(h)], st_ref[b, h].astype(BF16)) * xi_ref[h] for b, h in pairs]
    upd = [_dot((rk_ref[b, :, sl(h)].astype(F32) * zeta_ref[h]).T.astype(BF16), rv_ref[b, :, sl(h)])
           for b, h in pairs]
    for i, (b, h) in enumerate(pairs):
        o = _dot(inner[i], rv_ref[b, :, sl(h)]) + cross[i]
        st_ref[b, h] = gch_ref[h] * st_ref[b, h] + upd[i]
        ms = jnp.mean(o * o, axis=-1, keepdims=True)
        o_ref[b, :, sl(h)] = (o * lax.rsqrt(ms + RMS_EPS)) * g_ref[:, sl(h)]

    @pl.when(n == n_last)
    def _():
        sfin_ref[...] = st_ref[...]


def _ret_tables(c):
    log_g = jnp.log(1.0 - 2.0 ** (-5.0 - jnp.arange(RET_HEADS, dtype=F32)))
    i = jnp.arange(c, dtype=F32)
    diff = i[:, None] - i[None, :]
    dmask = jnp.where(diff[None] >= 0, jnp.exp(log_g[:, None, None] * jnp.maximum(diff, 0.0)[None]), 0.0)
    zeta = jnp.exp(log_g[:, None] * (c - 1 - i)[None, :])
    xi = jnp.exp(log_g[:, None] * (i + 1)[None, :])
    gch = jnp.exp(log_g * c)
    bc = lambda t: jnp.broadcast_to(t[:, :, None], (RET_HEADS, c, LANES))
    return dmask, bc(zeta), bc(xi), jnp.broadcast_to(gch[:, None, None], (RET_HEADS, 1, LANES))


def _ret_call(rq, rk, rv, s0, ret_g, t_real, c):
    B, T, _ = rq.shape
    nc = T // c
    dmask, zeta, xi, gch = _ret_tables(c)
    bb = RET_BATCH_ROWS if B % RET_BATCH_ROWS == 0 else 1
    tok =pl.BlockSpec((bb, c, RET_WIDTH), lambda b, n: (b, n, 0))
    st = pl.BlockSpec((bb, RET_HEADS, RET_DK, RET_DV), lambda b, n: (b, 0, 0, 0))
    tab = lambda s: pl.BlockSpec(s, lambda b, n: (0,) * len(s))
    return pl.pallas_call(
        functools.partial(_ret_kernel, n_last=t_real // c - 1), grid=(B // bb, nc),
        in_specs=[tok, tok, tok, st, tab((RET_HEADS, c, c)), tab((RET_HEADS, c, LANES)),
                  tab((RET_HEADS, c, LANES)), tab((RET_HEADS, 1, LANES)), tab((1, RET_WIDTH))],
        out_specs=(tok, st),
        out_shape=(jax.ShapeDtypeStruct((B, T, RET_WIDTH), F32),
                   jax.ShapeDtypeStruct((B, RET_HEADS, RET_DK, RET_DV), F32)),
        scratch_shapes=[pltpu.VMEM((bb, RET_HEADS, RET_DK, RET_DV), F32)],
        compiler_params=pltpu.CompilerParams(dimension_semantics=("parallel", "arbitrary"),
                                             vmem_limit_bytes=VMEM_LIMIT),
        name="ret",
    )(rq, rk, rv, s0, dmask, zeta, xi, gch, ret_g)


def _out_kernel(h_ref, attn_ref, ag_ref, ret_ref, rg_ref, p_ref, wo_ref, pg_ref, pp_ref, png_ref, o_ref):
    ag = ag_ref[...]
    rg = rg_ref[...]
    mix_a = (attn_ref[...] * (ag * jax.nn.sigmoid(ag))).astype(BF16)
    mix_r = (ret_ref[...] * (rg * jax.nn.sigmoid(rg))).astype(BF16)
    h1 = h_ref[...] + _dot(mix_a, wo_ref[0:ATTN_WIDTH, :]) + _dot(mix_r, wo_ref[ATTN_WIDTH:, :])
    ms = jnp.mean(h1 * h1, axis=-1, keepdims=True)
    hn = ((h1 * lax.rsqrt(ms + RMS_EPS)) * png_ref[...]).astype(BF16)
    gate = jax.nn.sigmoid(_dot(hn, pg_ref[...]))
    o_ref[...] = h1 + gate * _dot(p_ref[...].astype(BF16), pp_ref[...])


def _out_call(h, attn, ag, ret, rg, p, wo, pgate, pproj, png, tm):
    B, T, _ = h.shape
    tok = lambda width: pl.BlockSpec((None, tm, width), lambda b, i: (b, i, 0))
    const = lambda s: pl.BlockSpec(s, lambda b, i: (0, 0))
    return pl.pallas_call(
        _out_kernel, grid=(B, T // tm),
        in_specs=[tok(D_MODEL), tok(ATTN_WIDTH), tok(ATTN_WIDTH), tok(RET_WIDTH), tok(RET_WIDTH), tok(PLE_DIM),
                  const((D_MODEL, D_MODEL)), const((D_MODEL, D_MODEL)), const((PLE_DIM, D_MODEL)),
                  const((1, D_MODEL))],
        out_specs=tok(D_MODEL),
        out_shape=jax.ShapeDtypeStruct((B, T, D_MODEL), F32),
        compiler_params=pltpu.CompilerParams(dimension_semantics=("parallel", "parallel"),
                                             vmem_limit_bytes=VMEM_LIMIT),
        name="out",
    )(h, attn, ag, ret, rg, p, wo, pgate, pproj, png)


def _rope_tables(pos0, t):
    half = RET_DK // 2
    inv = ROPE_BASE ** (-jnp.arange(half, dtype=F32) / half)
    ang = (pos0 + jnp.arange(t, dtype=jnp.int32)).astype(F32)[:, None] * inv[None, :]
    cos, sin = jnp.cos(ang), jnp.sin(ang)
    return jnp.concatenate([cos, cos], axis=1), jnp.concatenate([-sin, sin], axis=1)


def _group_mean_matrix(width, group):
    i = jnp.arange(width) // group
    return jnp.where(i[:, None] == i[None, :], 1.0 / group, 0.0).astype(BF16)


def _strict_lower(n):
    return (jnp.arange(n)[None, :] < jnp.arange(n)[:, None]).astype(BF16)


def _key_rows(n_past, t):
    return -(-(n_past + t) // KEY_BLOCK) * KEY_BLOCK


def _layer(h, p, layer, depth, n_past, t_real, key_bufs, s0, lw, consts, tm, chunk, lq):
    w, wo, ng, qg, kg, ret_g, pproj, pgate, png = lw
    gq, gk, tri = consts
    B, T, _ = h.shape
    cosf, sinf = _rope_tables(n_past, T)
    buf_shape = (depth, B, _key_rows(n_past, T), LANES)
    qT, kn_buf, v_buf, ag, iqT, ikw_buf, ikwT, rq, rk, rv, rg, kidx_buf = _proj_call(
        h, ng, w, qg, kg, gq, gk, cosf, sinf, tm, key_bufs, buf_shape, layer, n_past)
    ksel = min(TOPK_MAX, (n_past + t_real) // 4)
    attn = _attn_call(iqT, qT, ikwT, ikw_buf, kn_buf, v_buf, tri, layer, n_past, t_real, ksel, lq)
    ret, s_new = _ret_call(rq, rk, rv, s0, ret_g, t_real, chunk)
    h_new = _out_call(h, attn, ag, ret, rg, p, wo, pgate, pproj, png, tm)
    return h_new, (kn_buf, v_buf, ikw_buf, kidx_buf), s_new


def _permute_w_in(w_in):
    depth = w_in.shape[0]
    a = w_in[:, :, 0:1280]
    iq = w_in[:, :, 1280:1792]
    ik_iw = w_in[:, :, 1792:1864]
    r = w_in[:, :, 1864:3912]
    pad = jnp.zeros((depth, D_MODEL, W_COLS - C_IKW - 72), w_in.dtype)
    return jnp.concatenate([a, iq, r, ik_iw, pad], axis=2).astype(BF16)


def kernel(x_prompt, x_sample, cache_k, cache_v, cache_k_idx, state_ret, p_prompt, p_sample,
           w_in, w_out, norm_g, q_norm_g, k_norm_g, ret_norm_g, ple_proj, ple_gate, ple_norm_g):
    depth = w_in.shape[0]
    b_p, t_p, _ = x_prompt.shape
    b_s, t_s, _ = x_sample.shape
    past = cache_k.shape[2]
    t_s_pad = -(-t_s // LANES) * LANES

    w_perm = _permute_w_in(w_in)
    wo = w_out.astype(BF16)
    pproj = ple_proj.astype(BF16)
    pgate = ple_gate.astype(BF16)
    qg = jnp.tile(q_norm_g, (1, N_HEADS))[:, None, :]
    kg = jnp.tile(k_norm_g, (1, N_KV_HEADS))[:, None, :]
    consts = (_group_mean_matrix(ATTN_WIDTH, HEAD_DIM), _group_mean_matrix(KV_WIDTH, HEAD_DIM),
              _strict_lower(KEY_BLOCK))

    pad_t = lambda a, axis: jnp.pad(a, [(0, t_s_pad - t_s) if d == axis else (0, 0) for d in range(a.ndim)])
    h_p = x_prompt
    h_s = pad_t(x_sample, 1)
    p_s = pad_t(p_sample, 2)
    r0 = jnp.zeros((b_p, RET_HEADS, RET_DK, RET_DV), F32)
    grow = lambda a: jnp.pad(a, ((0, 0), (0, 0), (0, _key_rows(past, t_s_pad) - past), (0, LANES - a.shape[-1])))
    bufs_s = (grow(cache_k.reshape(depth, b_s, past, KV_WIDTH)), grow(cache_v.reshape(depth, b_s, past, KV_WIDTH)),
              grow(cache_k_idx), jnp.zeros((depth, b_s, t_s_pad, IDX_DIM), F32))
    bufs_p = None

    states_p, states_s = [], []
    for i in range(depth):
        lw = (w_perm[i], wo[i], norm_g[i][None], qg[i], kg[i], ret_norm_g[i][None], pproj[i], pgate[i],
              ple_norm_g[i][None])
        h_p, bufs_p, rp = _layer(h_p, p_prompt[i], i, depth, 0, t_p, bufs_p, r0, lw, consts,
                                 PROMPT_ROWS, 2 * CHUNK, 2 * LANES)
        h_s, bufs_s, rs = _layer(h_s, p_s[i], i, depth, past, t_s, bufs_s, state_ret[i], lw, consts,
                                 LANES, CHUNK, LANES)
        states_p.append(rp)
        states_s.append(rs)

    heads = lambda a: a.reshape(a.shape[:-1] + (N_KV_HEADS, HEAD_DIM))
    new_s = [b[:, :, past:past + t_s] for b in bufs_s[:2]]
    return (h_p, h_s[:, :t_s],
            heads(bufs_p[0]), heads(bufs_p[1]), bufs_p[3], jnp.stack(states_p),
            heads(new_s[0]), heads(new_s[1]), bufs_s[3][:, :, :t_s], jnp.stack(states_s))
```

```python
import functools

import jax
import jax.numpy as jnp
from jax import lax
from jax.experimental import pallas as pl
from jax.experimental.pallas import tpu as pltpu

F32 = jnp.float32
BF16 = jnp.bfloat16

D_MODEL = 1024
CHUNK = 64
ATTN_WIDTH = 512
RET_WIDTH = 512
N_HEADS = 8
HEAD_DIM = 64
N_KV_HEADS = 2
KV_GROUP = N_HEADS // N_KV_HEADS
KV_WIDTH = N_KV_HEADS * HEAD_DIM
IDX_HEADS = 8
IDX_DIM = 64
TOPK_MAX = 256
RET_HEADS = 4
RET_DK = 128
RET_DV = 128
PLE_DIM = 256
RMS_EPS = 1e-6
NEG_INF = -1e30
ATTN_SCALE = HEAD_DIM ** -0.5
IDX_SCALE = IDX_DIM ** -0.5
IDX_HEAD_SCALE = IDX_HEADS ** -0.5
ROPE_BASE = 10000.0
LOG2_E = 1.4426950408889634

LANES = 128
SUBLANES = 8
MXU_WIDTH = 256
KEY_BLOCK = MXU_WIDTH
PROMPT_ROWS = 512
RET_BATCH_ROWS = 4
DEN_ROWS = 16
INT_MIN = -(2 ** 31)
I16_MIN = -(2 ** 15)
I16_ROWS = 16
VMEM_LIMIT = 56 * 1024 * 1024

C_Q = 0
C_K = 512
C_V = 640
C_AG = 768
C_IQ = 1280
C_RQ = 1792
C_RK = 2304
C_RV = 2816
C_RG = 3328
C_IKW = 3840
W_COLS = 3968


def _dot(a, b):
    return jnp.dot(a, b, preferred_element_type=F32)


def _dot_nt(a, b):
    return lax.dot_general(a, b, (((1,), (1,)), ((), ())), preferred_element_type=F32)


N_PROJ_OUTS = 14


def _proj_kernel(x_ref, ng_ref, w_ref, qg_ref, kg_ref, gq_ref, gk_ref, cos_ref, sin_ref, *refs):
    (qT_ref, kn_ref, v_ref, ag_ref, iqT_ref, ikw_ref, ikwT_ref,
     rq_ref, rk_ref, rv_ref, rg_ref, kidx_ref, kh_ref, vh_ref) = refs[-N_PROJ_OUTS:]
    x = x_ref[...]
    ms = jnp.mean(x * x, axis=-1, keepdims=True)
    xn = ((x * lax.rsqrt(ms + RMS_EPS)) * ng_ref[...]).astype(BF16)

    def proj(c0, c1):
        return _dot(xn, w_ref[:, c0:c1])

    def head_ms(u, gmat_ref):
        sq = u * u
        hi = sq.astype(BF16)
        lo = (sq - hi.astype(F32)).astype(BF16)
        return _dot(hi, gmat_ref[...]) + _dot(lo, gmat_ref[...])

    uq = proj(C_Q, C_K)
    qn = (uq * lax.rsqrt(head_ms(uq, gq_ref) + RMS_EPS)) * qg_ref[...]
    qT_ref[...] = (qn * (ATTN_SCALE * LOG2_E)).T.astype(BF16)

    uk = proj(C_K, C_V)
    kn = (uk * lax.rsqrt(head_ms(uk, gk_ref) + RMS_EPS)) * kg_ref[...]
    uv = proj(C_V, C_AG)
    kn_ref[...] = kn
    v_ref[...] = uv
    for g in range(N_KV_HEADS):
        kh_ref[:, g, :] = kn[:, g * HEAD_DIM:(g + 1) * HEAD_DIM]
        vh_ref[:, g, :] = uv[:, g * HEAD_DIM:(g + 1) * HEAD_DIM]
    ag_ref[...] = proj(C_AG, C_IQ)
    iqT_ref[...] = proj(C_IQ, C_RQ).T.astype(BF16)
    ikw = proj(C_IKW, W_COLS)
    ikw_ref[...] = ikw
    kidx_ref[...] = ikw[:, 0:IDX_DIM]
    ikwT_ref[...] = ikw.T

    cosf = cos_ref[...]
    sinf = sin_ref[...]

    def rotary(u):
        parts = []
        for h in range(RET_HEADS):
            xh = u[:, h * RET_DK:(h + 1) * RET_DK]
            parts.append(xh * cosf + pltpu.roll(xh, RET_DK // 2, 1) * sinf)
        return jnp.concatenate(parts, axis=1)

    rq_ref[...] = rotary(proj(C_RQ, C_RK)).astype(BF16)
    rk_ref[...] = (rotary(proj(C_RK, C_RV)) * (RET_DK ** -0.5)).astype(BF16)
    rv_ref[...] = proj(C_RV, C_RG).astype(BF16)
    rg_ref[...] = proj(C_RG, C_IKW)


def _proj_call(x, ng, w, qg, kg, gq, gk, cosf, sinf, tm, key_bufs, buf_shape, layer, n_past):
    B, T, _ = x.shape
    nt = T // tm
    assert n_past % tm == 0
    row = lambda b, i: (b, i, 0)
    col = lambda b, i: (b, 0, i)
    const = lambda b, i: (0, 0)
    tok = lambda width: pl.BlockSpec((None, tm, width), row)
    tokT = lambda width: pl.BlockSpec((None, width, tm), col)
    buf = pl.BlockSpec((None, None, tm, LANES), lambda b, i: (layer, b, n_past // tm + i, 0))
    buf_sds = jax.ShapeDtypeStruct(buf_shape, F32)
    out_shape = (
        jax.ShapeDtypeStruct((B, ATTN_WIDTH, T), BF16),
        buf_sds,
        buf_sds,
        jax.ShapeDtypeStruct((B, T, ATTN_WIDTH), F32),
        jax.ShapeDtypeStruct((B, ATTN_WIDTH, T), BF16),
        buf_sds,
        jax.ShapeDtypeStruct((B, LANES, T), F32),
        jax.ShapeDtypeStruct((B, T, RET_WIDTH), BF16),
        jax.ShapeDtypeStruct((B, T, RET_WIDTH), BF16),
        jax.ShapeDtypeStruct((B, T, RET_WIDTH), BF16),
        jax.ShapeDtypeStruct((B, T, RET_WIDTH), F32),
        jax.ShapeDtypeStruct((buf_shape[0], B, T, IDX_DIM), F32),
        jax.ShapeDtypeStruct((buf_shape[0], B, T, N_KV_HEADS, HEAD_DIM), F32),
        jax.ShapeDtypeStruct((buf_shape[0], B, T, N_KV_HEADS, HEAD_DIM), F32),
    )
    kidx = pl.BlockSpec((None, None, tm, IDX_DIM), lambda b, i: (layer, b, i, 0))
    per_head = pl.BlockSpec((None, None, tm, N_KV_HEADS, HEAD_DIM), lambda b, i: (layer, b, i, 0, 0))
    out_specs = (tokT(ATTN_WIDTH), buf, buf, tok(ATTN_WIDTH), tokT(ATTN_WIDTH),
                 buf, tokT(LANES), tok(RET_WIDTH), tok(RET_WIDTH), tok(RET_WIDTH), tok(RET_WIDTH),
                 kidx, per_head, per_head)
    n_in = 9
    extra_in, extra_specs, aliases = (), [], {}
    if key_bufs is not None:
        extra_in = tuple(key_bufs)
        extra_specs = [pl.BlockSpec(memory_space=pl.ANY)] * len(extra_in)
        aliases = {n_in: 1, n_in + 1: 2, n_in + 2: 5, n_in + 3: 11, n_in + 4: 12, n_in + 5: 13}
    in_specs = [
        tok(D_MODEL),
        pl.BlockSpec((1, D_MODEL), const),
        pl.BlockSpec((D_MODEL, W_COLS), const),
        pl.BlockSpec((1, ATTN_WIDTH), const),
        pl.BlockSpec((1, KV_WIDTH), const),
        pl.BlockSpec((ATTN_WIDTH, ATTN_WIDTH), const),
        pl.BlockSpec((KV_WIDTH, KV_WIDTH), const),
        pl.BlockSpec((tm, LANES), lambda b, i: (i, 0)),
        pl.BlockSpec((tm, LANES), lambda b, i: (i, 0)),
    ]
    return pl.pallas_call(
        _proj_kernel, grid=(B, nt), in_specs=in_specs + extra_specs, out_specs=out_specs,
        out_shape=out_shape, input_output_aliases=aliases,
        compiler_params=pltpu.CompilerParams(dimension_semantics=("parallel", "parallel"),
                                             vmem_limit_bytes=VMEM_LIMIT),
        name="proj",
    )(x, ng, w, qg, kg, gq, gk, cosf, sinf, *extra_in)


def _loop_pairs(n, body, init):
    def quad(i, carry):
        for r in range(4):
            carry = body(4 * i + r, carry)
        return carry

    def pair(i, carry):
        return body(2 * i + 1, body(2 * i, carry))

    carry = lax.fori_loop(0, n >> 2, quad, init)
    carry = lax.fori_loop((n >> 2) << 1, n >> 1, pair, carry)
    return lax.fori_loop(n & ~1, n, body, carry)


def _attn_kernel(iqT_ref, qT_ref, ikwT_ref, ikw_ref, kn_ref, v_ref, tri_ref, out_ref,
                 ikw16_ref, kn16_ref, vaug_ref, sc_ref, hi_ref, lo_ref, bias_ref, lg_ref, acc_ref,
                 *, pos0, ksel, nkb_max, lq):
    q0 = pos0 + pl.program_id(1) * lq
    lane = lax.broadcasted_iota(jnp.int32, (1, lq), 1)
    limit = (((q0 + lane) >> 6) + 1) << 6
    last_limit = (((q0 + lq - 1) >> 6) + 1) << 6
    nkb = jnp.minimum((last_limit + KEY_BLOCK - 1) // KEY_BLOCK, nkb_max)
    sub = lax.broadcasted_iota(jnp.int32, (KEY_BLOCK, 1), 0)

    @pl.when(pl.program_id(1) == 0)
    def _():
        ones = jnp.ones((DEN_ROWS, KEY_BLOCK), F32)

        def stage_body(kb, carry):
            off = pl.multiple_of(kb * KEY_BLOCK, KEY_BLOCK)
            ikw16_ref[pl.ds(off, KEY_BLOCK), :] = ikw_ref[pl.ds(off, KEY_BLOCK), :].astype(BF16)
            kn16_ref[pl.ds(off, KEY_BLOCK), :] = kn_ref[pl.ds(off, KEY_BLOCK), :].astype(BF16)
            vT = v_ref[pl.ds(off, KEY_BLOCK), :].T
            for g in range(N_KV_HEADS):
                vaug_ref[g, :, pl.ds(off, KEY_BLOCK)] = jnp.concatenate(
                    [vT[g * HEAD_DIM:(g + 1) * HEAD_DIM, :], ones], axis=0).astype(BF16)
            return carry

        lax.fori_loop(0, nkb_max, stage_body, 0)

    iqT = iqT_ref[...]
    qT = qT_ref[...]
    zpad = jnp.zeros((HEAD_DIM, lq), BF16)
    rhs_idx = [jnp.concatenate([iqT[h * IDX_DIM:(h + 1) * IDX_DIM, :], zpad], axis=0)
               for h in range(IDX_HEADS)]
    wT = ikwT_ref[...]
    wrows = [wT[IDX_DIM + h:IDX_DIM + h + 1, :] * (IDX_SCALE * IDX_HEAD_SCALE) for h in range(IDX_HEADS)]

    hp = max(1, MXU_WIDTH // lq)
    side_by_side = lambda ops: [jnp.concatenate(ops[i:i + hp], axis=1) for i in range(0, len(ops), hp)]

    def dot_heads(lhs, rhs_wide):
        res = _dot(lhs, rhs_wide)
        return [res[:, j * lq:(j + 1) * lq] for j in range(hp)]

    rhs_idx = side_by_side(rhs_idx)

    def score_body(kb, carry):
        off = pl.multiple_of(kb * KEY_BLOCK, KEY_BLOCK)
        kk = ikw16_ref[pl.ds(off, KEY_BLOCK), :]
        acc = None
        for i, rhs in enumerate(rhs_idx):
            for j, s in enumerate(dot_heads(kk, rhs)):
                term = jnp.maximum(s, 0.0) * wrows[i * hp + j]
                acc = term if acc is None else acc + term
        bits = pltpu.bitcast(acc, jnp.int32)
        key = bits ^ ((bits >> 31) & 0x7FFFFFFF)
        key = jnp.where(key == -1, 0, key)
        key = jnp.where(off + sub < limit, key, INT_MIN)
        sc_ref[pl.ds(off, KEY_BLOCK), :] = key
        hi_ref[pl.ds(off, KEY_BLOCK), :] = (key >> 16).astype(jnp.int16)
        lo_ref[pl.ds(off, KEY_BLOCK), :] = ((key & 0xFFFF) + I16_MIN).astype(jnp.int16)
        return carry

    _loop_pairs(nkb, score_body, 0)

    kf = float(ksel)

    def count16(ref, pred_fn):
        def body(kb, c):
            off = pl.multiple_of(kb * KEY_BLOCK, KEY_BLOCK)
            hit = jnp.where(pred_fn(ref[pl.ds(off, KEY_BLOCK), :]), jnp.int16(1), jnp.int16(0))
            parts = [hit[r:r + I16_ROWS, :] for r in range(0, KEY_BLOCK, I16_ROWS)]
            while len(parts) > 1:
                parts = [parts[r] + parts[r + 1] for r in range(0, len(parts), 2)]
            return c + parts[0]
        part = _loop_pairs(nkb, body, jnp.zeros((I16_ROWS, lq), jnp.int16))
        return jnp.sum(part.astype(F32), axis=0, keepdims=True)

    def kth_largest16(ref, k_row):
        def bit_body(i, t):
            cand = t + jnp.left_shift(jnp.int32(1), 15 - i)
            cand16 = cand.astype(jnp.int16)
            c = count16(ref, lambda blk: blk >= cand16)
            return jnp.where(c >= k_row, cand, t)
        return lax.fori_loop(0, 16, bit_body, jnp.full((1, lq), I16_MIN, jnp.int32))

    t_hi = kth_largest16(hi_ref, kf)
    t_hi16 = t_hi.astype(jnp.int16)
    k_lo = kf - count16(hi_ref, lambda blk: blk > t_hi16)

    def plane_body(kb, carry):
        off = pl.multiple_of(kb * KEY_BLOCK, KEY_BLOCK)
        lo_ref[pl.ds(off, KEY_BLOCK), :] = jnp.where(
            hi_ref[pl.ds(off, KEY_BLOCK), :] == t_hi16, lo_ref[pl.ds(off, KEY_BLOCK), :], jnp.int16(I16_MIN))
        return carry

    lax.fori_loop(0, nkb, plane_body, 0)
    t_lo = kth_largest16(lo_ref, k_lo)
    t_lo16 = t_lo.astype(jnp.int16)
    thr = (t_hi << 16) | (t_lo - I16_MIN)
    c_gt_lo = count16(lo_ref, lambda blk: blk > t_lo16)
    need = jnp.where(thr == INT_MIN, 0.0, k_lo - c_gt_lo)

    def qpad(h):
        qh = qT[h * HEAD_DIM:(h + 1) * HEAD_DIM, :]
        return jnp.concatenate([qh, zpad] if h // KV_GROUP == 0 else [zpad, qh], axis=0)

    rhs_q = side_by_side([qpad(h) for h in range(N_HEADS)])
    groups_per_kv = KV_GROUP // hp

    acc_ref[...] = jnp.zeros_like(acc_ref)
    max_init = tuple(jnp.full((SUBLANES, lq), NEG_INF, F32) for _ in range(KV_GROUP))

    def logits_group(g, off, bias, maxes):
        kb16 = kn16_ref[pl.ds(off, KEY_BLOCK), :]
        new = []
        for i in range(groups_per_kv):
            for j, raw in enumerate(dot_heads(kb16, rhs_q[g * groups_per_kv + i])):
                hl = i * hp + j
                lg = raw + bias
                lg_ref[g * KV_GROUP + hl, pl.ds(off, KEY_BLOCK), :] = lg
                new.append(jnp.maximum(maxes[hl], jnp.max(lg.reshape(KEY_BLOCK // SUBLANES, SUBLANES, lq), axis=0)))
        return tuple(new)

    def pv_group(g, off, ms):
        vaug = vaug_ref[g, :, pl.ds(off, KEY_BLOCK)]
        for i in range(groups_per_kv):
            hls = range(i * hp, (i + 1) * hp)
            p = [jnp.exp2(lg_ref[g * KV_GROUP + hl, pl.ds(off, KEY_BLOCK), :] - ms[hl]).astype(BF16) for hl in hls]
            for hl, upd in zip(hls, dot_heads(vaug, jnp.concatenate(p, axis=1))):
                acc_ref[g * KV_GROUP + hl] += upd

    def body_a(kb, carry):
        run_eq = carry[0]
        off = pl.multiple_of(kb * KEY_BLOCK, KEY_BLOCK)
        blk = sc_ref[pl.ds(off, KEY_BLOCK), :]
        eqf = jnp.where(blk == thr, 1.0, 0.0)
        rank = run_eq + _dot(tri_ref[...], eqf.astype(BF16))
        sel = jnp.where(blk > thr, 1.0, jnp.where(rank < need, eqf, 0.0))
        bias = (1.0 - sel) * NEG_INF
        bias_ref[pl.ds(off, KEY_BLOCK), :] = bias
        run_eq = run_eq + jnp.sum(eqf, axis=0, keepdims=True)
        return (run_eq,) + logits_group(0, off, bias, carry[1:])

    carry = _loop_pairs(nkb, body_a, (jnp.zeros((1, lq), F32),) + max_init)
    ms0 = [jnp.max(c, axis=0, keepdims=True) for c in carry[1:]]

    def body_b(kb, maxes):
        off = pl.multiple_of(kb * KEY_BLOCK, KEY_BLOCK)
        new = logits_group(1, off, bias_ref[pl.ds(off, KEY_BLOCK), :], maxes)
        pv_group(0, off, ms0)
        return new

    ms1 = [jnp.max(c, axis=0, keepdims=True) for c in _loop_pairs(nkb, body_b, max_init)]

    def body_c(kb, carry):
        pv_group(1, pl.multiple_of(kb * KEY_BLOCK, KEY_BLOCK), ms1)
        return carry

    _loop_pairs(nkb, body_c, 0)

    outs = []
    for h in range(N_HEADS):
        a = acc_ref[h]
        outs.append(a[0:HEAD_DIM, :] * (1.0 / a[HEAD_DIM:HEAD_DIM + 1, :]))
    out_ref[...] = jnp.concatenate(outs, axis=0).T


def _attn_call(iqT, qT, ikwT, ikw_keys, kn_keys, v_keys, tri, layer, pos0, t_real, ksel, lq):
    B, _, T = qT.shape
    Lk = ikw_keys.shape[2]
    qblk = lambda rows: pl.BlockSpec((None, rows, lq), lambda b, j: (b, 0, j))
    keys = pl.BlockSpec((None, None, Lk, LANES), lambda b, j: (layer, b, 0, 0))
    kern = functools.partial(_attn_kernel, pos0=pos0, ksel=ksel, nkb_max=Lk // KEY_BLOCK, lq=lq)
    return pl.pallas_call(
        kern, grid=(B, -(-t_real // lq)),
        in_specs=[qblk(ATTN_WIDTH), qblk(ATTN_WIDTH), qblk(LANES), keys, keys, keys,
                  pl.BlockSpec((KEY_BLOCK, KEY_BLOCK), lambda b, j: (0, 0))],
        out_specs=pl.BlockSpec((None, lq, ATTN_WIDTH), lambda b, j: (b, j, 0)),
        out_shape=jax.ShapeDtypeStruct((B, T, ATTN_WIDTH), F32),
        scratch_shapes=[pltpu.VMEM((Lk, LANES), BF16),
                        pltpu.VMEM((Lk, LANES), BF16),
                        pltpu.VMEM((N_KV_HEADS, HEAD_DIM + DEN_ROWS, Lk), BF16),
                        pltpu.VMEM((Lk, lq), jnp.int32),
                        pltpu.VMEM((Lk, lq), jnp.int16),
                        pltpu.VMEM((Lk, lq), jnp.int16),
                        pltpu.VMEM((Lk, lq), F32),
                        pltpu.VMEM((N_HEADS, Lk, lq), F32),
                        pltpu.VMEM((N_HEADS, HEAD_DIM + DEN_ROWS, lq), F32)],
        compiler_params=pltpu.CompilerParams(dimension_semantics=("arbitrary", "arbitrary"),
                                             vmem_limit_bytes=VMEM_LIMIT),
        name="attn",
    )(iqT, qT, ikwT, ikw_keys, kn_keys, v_keys, tri)


def _ret_kernel(rq_ref, rk_ref, rv_ref, s0_ref, dmask_ref, zeta_ref, xi_ref, gch_ref, g_ref,
                o_ref, sfin_ref, st_ref, *, n_last):
    n = pl.program_id(1)

    @pl.when(n == 0)
    def _():
        st_ref[...] = s0_ref[...]

    pairs = [(b, h) for b in range(rq_ref.shape[0]) for h in range(RET_HEADS)]
    sl = lambda h: slice(h * RET_DK, (h + 1) * RET_DK)
    inner = [(_dot_nt(rq_ref[b, :, sl(h)], rk_ref[b, :, sl(h)]) * dmask_ref[h]).astype(BF16) for b, h in pairs]
    cross = [_dot(rq_ref[b, :, sl(h)], st_ref[b, h].astype(BF16)) * xi_ref[h] for b, h in pairs]
    upd = [_dot((rk_ref[b, :, sl(h)].astype(F32) * zeta_ref[h]).T.astype(BF16), rv_ref[b, :, sl(h)])
           for b, h in pairs]
    for i, (b, h) in enumerate(pairs):
        o = _dot(inner[i], rv_ref[b, :, sl(h)]) + cross[i]
        st_ref[b, h] = gch_ref[h] * st_ref[b, h] + upd[i]
        ms = jnp.mean(o * o, axis=-1, keepdims=True)
        o_ref[b, :, sl(h)] = (o * lax.rsqrt(ms + RMS_EPS)) * g_ref[:, sl(h)]

    @pl.when(n == n_last)
    def _():
        sfin_ref[...] = st_ref[...]


def _ret_tables(c):
    log_g = jnp.log(1.0 - 2.0 ** (-5.0 - jnp.arange(RET_HEADS, dtype=F32)))
    i = jnp.arange(c, dtype=F32)
    diff = i[:, None] - i[None, :]
    dmask = jnp.where(diff[None] >= 0, jnp.exp(log_g[:, None, None] * jnp.maximum(diff, 0.0)[None]), 0.0)
    zeta = jnp.exp(log_g[:, None] * (c - 1 - i)[None, :])
    xi = jnp.exp(log_g[:, None] * (i + 1)[None, :])
    gch = jnp.exp(log_g * c)
    bc = lambda t: jnp.broadcast_to(t[:, :, None], (RET_HEADS, c, LANES))
    return dmask, bc(zeta), bc(xi), jnp.broadcast_to(gch[:, None, None], (RET_HEADS, 1, LANES))


def _ret_call(rq, rk, rv, s0, ret_g, t_real, c):
    B, T, _ = rq.shape
    nc = T // c
    dmask, zeta, xi, gch = _ret_tables(c)
    bb = RET_BATCH_ROWS if B % RET_BATCH_ROWS == 0 else 1
    tok =pl.BlockSpec((bb, c, RET_WIDTH), lambda b, n: (b, n, 0))
    st = pl.BlockSpec((bb, RET_HEADS, RET_DK, RET_DV), lambda b, n: (b, 0, 0, 0))
    tab = lambda s: pl.BlockSpec(s, lambda b, n: (0,) * len(s))
    return pl.pallas_call(
        functools.partial(_ret_kernel, n_last=t_real // c - 1), grid=(B // bb, nc),
        in_specs=[tok, tok, tok, st, tab((RET_HEADS, c, c)), tab((RET_HEADS, c, LANES)),
                  tab((RET_HEADS, c, LANES)), tab((RET_HEADS, 1, LANES)), tab((1, RET_WIDTH))],
        out_specs=(tok, st),
        out_shape=(jax.ShapeDtypeStruct((B, T, RET_WIDTH), F32),
                   jax.ShapeDtypeStruct((B, RET_HEADS, RET_DK, RET_DV), F32)),
        scratch_shapes=[pltpu.VMEM((bb, RET_HEADS, RET_DK, RET_DV), F32)],
        compiler_params=pltpu.CompilerParams(dimension_semantics=("parallel", "arbitrary"),
                                             vmem_limit_bytes=VMEM_LIMIT),
        name="ret",
    )(rq, rk, rv, s0, dmask, zeta, xi, gch, ret_g)


def _out_kernel(h_ref, attn_ref, ag_ref, ret_ref, rg_ref, p_ref, wo_ref, pg_ref, pp_ref, png_ref, o_ref):
    ag = ag_ref[...]
    rg = rg_ref[...]
    mix_a = (attn_ref[...] * (ag * jax.nn.sigmoid(ag))).astype(BF16)
    mix_r = (ret_ref[...] * (rg * jax.nn.sigmoid(rg))).astype(BF16)
    h1 = h_ref[...] + _dot(mix_a, wo_ref[0:ATTN_WIDTH, :]) + _dot(mix_r, wo_ref[ATTN_WIDTH:, :])
    ms = jnp.mean(h1 * h1, axis=-1, keepdims=True)
    hn = ((h1 * lax.rsqrt(ms + RMS_EPS)) * png_ref[...]).astype(BF16)
    gate = jax.nn.sigmoid(_dot(hn, pg_ref[...]))
    o_ref[...] = h1 + gate * _dot(p_ref[...].astype(BF16), pp_ref[...])


def _out_call(h, attn, ag, ret, rg, p, wo, pgate, pproj, png, tm):
    B, T, _ = h.shape
    tok = lambda width: pl.BlockSpec((None, tm, width), lambda b, i: (b, i, 0))
    const = lambda s: pl.BlockSpec(s, lambda b, i: (0, 0))
    return pl.pallas_call(
        _out_kernel, grid=(B, T // tm),
        in_specs=[tok(D_MODEL), tok(ATTN_WIDTH), tok(ATTN_WIDTH), tok(RET_WIDTH), tok(RET_WIDTH), tok(PLE_DIM),
                  const((D_MODEL, D_MODEL)), const((D_MODEL, D_MODEL)), const((PLE_DIM, D_MODEL)),
                  const((1, D_MODEL))],
        out_specs=tok(D_MODEL),
        out_shape=jax.ShapeDtypeStruct((B, T, D_MODEL), F32),
        compiler_params=pltpu.CompilerParams(dimension_semantics=("parallel", "parallel"),
                                             vmem_limit_bytes=VMEM_LIMIT),
        name="out",
    )(h, attn, ag, ret, rg, p, wo, pgate, pproj, png)


def _rope_tables(pos0, t):
    half = RET_DK // 2
    inv = ROPE_BASE ** (-jnp.arange(half, dtype=F32) / half)
    ang = (pos0 + jnp.arange(t, dtype=jnp.int32)).astype(F32)[:, None] * inv[None, :]
    cos, sin = jnp.cos(ang), jnp.sin(ang)
    return jnp.concatenate([cos, cos], axis=1), jnp.concatenate([-sin, sin], axis=1)


def _group_mean_matrix(width, group):
    i = jnp.arange(width) // group
    return jnp.where(i[:, None] == i[None, :], 1.0 / group, 0.0).astype(BF16)


def _strict_lower(n):
    return (jnp.arange(n)[None, :] < jnp.arange(n)[:, None]).astype(BF16)


def _key_rows(n_past, t):
    return -(-(n_past + t) // KEY_BLOCK) * KEY_BLOCK


def _layer(h, p, layer, depth, n_past, t_real, key_bufs, s0, lw, consts, tm, chunk, lq):
    w, wo, ng, qg, kg, ret_g, pproj, pgate, png = lw
    gq, gk, tri = consts
    B, T, _ = h.shape
    cosf, sinf = _rope_tables(n_past, T)
    buf_shape = (depth, B, _key_rows(n_past, T), LANES)
    qT, kn_buf, v_buf, ag, iqT, ikw_buf, ikwT, rq, rk, rv, rg, kidx_buf, kh_buf, vh_buf = _proj_call(
        h, ng, w, qg, kg, gq, gk, cosf, sinf, tm, key_bufs, buf_shape, layer, n_past)
    ksel = min(TOPK_MAX, (n_past + t_real) // 4)
    attn = _attn_call(iqT, qT, ikwT, ikw_buf, kn_buf, v_buf, tri, layer, n_past, t_real, ksel, lq)
    ret, s_new = _ret_call(rq, rk, rv, s0, ret_g, t_real, chunk)
    h_new = _out_call(h, attn, ag, ret, rg, p, wo, pgate, pproj, png, tm)
    return h_new, (kn_buf, v_buf, ikw_buf, kidx_buf, kh_buf, vh_buf), s_new


def _permute_w_in(w_in):
    depth = w_in.shape[0]
    a = w_in[:, :, 0:1280]
    iq = w_in[:, :, 1280:1792]
    ik_iw = w_in[:, :, 1792:1864]
    r = w_in[:, :, 1864:3912]
    pad = jnp.zeros((depth, D_MODEL, W_COLS - C_IKW - 72), w_in.dtype)
    return jnp.concatenate([a, iq, r, ik_iw, pad], axis=2).astype(BF16)


def kernel(x_prompt, x_sample, cache_k, cache_v, cache_k_idx, state_ret, p_prompt, p_sample,
           w_in, w_out, norm_g, q_norm_g, k_norm_g, ret_norm_g, ple_proj, ple_gate, ple_norm_g):
    depth = w_in.shape[0]
    b_p, t_p, _ = x_prompt.shape
    b_s, t_s, _ = x_sample.shape
    past = cache_k.shape[2]
    t_s_pad = -(-t_s // LANES) * LANES

    w_perm = _permute_w_in(w_in)
    wo = w_out.astype(BF16)
    pproj = ple_proj.astype(BF16)
    pgate = ple_gate.astype(BF16)
    qg = jnp.tile(q_norm_g, (1, N_HEADS))[:, None, :]
    kg = jnp.tile(k_norm_g, (1, N_KV_HEADS))[:, None, :]
    consts = (_group_mean_matrix(ATTN_WIDTH, HEAD_DIM), _group_mean_matrix(KV_WIDTH, HEAD_DIM),
              _strict_lower(KEY_BLOCK))

    pad_t = lambda a, axis: jnp.pad(a, [(0, t_s_pad - t_s) if d == axis else (0, 0) for d in range(a.ndim)])
    h_p = x_prompt
    h_s = pad_t(x_sample, 1)
    p_s = pad_t(p_sample, 2)
    r0 = jnp.zeros((b_p, RET_HEADS, RET_DK, RET_DV), F32)
    grow = lambda a: jnp.pad(a, ((0, 0), (0, 0), (0, _key_rows(past, t_s_pad) - past), (0, LANES - a.shape[-1])))
    bufs_s = (grow(cache_k.reshape(depth, b_s, past, KV_WIDTH)), grow(cache_v.reshape(depth, b_s, past, KV_WIDTH)),
              grow(cache_k_idx), jnp.zeros((depth, b_s, t_s_pad, IDX_DIM), F32),
              jnp.zeros((depth, b_s, t_s_pad, N_KV_HEADS, HEAD_DIM), F32),
              jnp.zeros((depth, b_s, t_s_pad, N_KV_HEADS, HEAD_DIM), F32))
    bufs_p = None

    states_p, states_s = [], []
    for i in range(depth):
        lw = (w_perm[i], wo[i], norm_g[i][None], qg[i], kg[i], ret_norm_g[i][None], pproj[i], pgate[i],
              ple_norm_g[i][None])
        h_p, bufs_p, rp = _layer(h_p, p_prompt[i], i, depth, 0, t_p, bufs_p, r0, lw, consts,
                                 PROMPT_ROWS, 2 * CHUNK, 2 * LANES)
        h_s, bufs_s, rs = _layer(h_s, p_s[i], i, depth, past, t_s, bufs_s, state_ret[i], lw, consts,
                                 LANES, CHUNK, LANES)
        states_p.append(rp)
        states_s.append(rs)

    return (h_p, h_s[:, :t_s],
            bufs_p[4], bufs_p[5], bufs_p[3], jnp.stack(states_p),
            bufs_s[4][:, :, :t_s], bufs_s[5][:, :, :t_s], bufs_s[3][:, :, :t_s], jnp.stack(states_s))
```

```python
import functools

import jax
import jax.numpy as jnp
from jax import lax
from jax.experimental import pallas as pl
from jax.experimental.pallas import tpu as pltpu

F32 = jnp.float32
BF16 = jnp.bfloat16

D_MODEL = 1024
CHUNK = 64
ATTN_WIDTH = 512
RET_WIDTH = 512
N_HEADS = 8
HEAD_DIM = 64
N_KV_HEADS = 2
KV_GROUP = N_HEADS // N_KV_HEADS
KV_WIDTH = N_KV_HEADS * HEAD_DIM
IDX_HEADS = 8
IDX_DIM = 64
TOPK_MAX = 256
RET_HEADS = 4
RET_DK = 128
RET_DV = 128
PLE_DIM = 256
RMS_EPS = 1e-6
NEG_INF = -1e30
ATTN_SCALE = HEAD_DIM ** -0.5
IDX_SCALE = IDX_DIM ** -0.5
IDX_HEAD_SCALE = IDX_HEADS ** -0.5
ROPE_BASE = 10000.0
LOG2_E = 1.4426950408889634

LANES = 128
SUBLANES = 8
MXU_WIDTH = 256
KEY_BLOCK = MXU_WIDTH
PROMPT_ROWS = 512
RET_BATCH_ROWS = 4
DEN_ROWS = 16
INT_MIN = -(2 ** 31)
I16_MIN = -(2 ** 15)
I16_ROWS = 16
VMEM_LIMIT = 56 * 1024 * 1024

C_Q = 0
C_K = 512
C_V = 640
C_AG = 768
C_IQ = 1280
C_RQ = 1792
C_RK = 2304
C_RV = 2816
C_RG = 3328
C_IKW = 3840
W_COLS = 3968


def _dot(a, b):
    return jnp.dot(a, b, preferred_element_type=F32)


def _dot_nt(a, b):
    return lax.dot_general(a, b, (((1,), (1,)), ((), ())), preferred_element_type=F32)


N_PROJ_OUTS = 12


def _proj_kernel(x_ref, ng_ref, w_ref, qg_ref, kg_ref, gq_ref, gk_ref, cos_ref, sin_ref, *refs):
    (qT_ref, kn_ref, v_ref, ag_ref, iqT_ref, ikw_ref, ikwT_ref,
     rq_ref, rk_ref, rv_ref, rg_ref, kidx_ref) = refs[-N_PROJ_OUTS:]
    x = x_ref[...]
    ms = jnp.mean(x * x, axis=-1, keepdims=True)
    xn = ((x * lax.rsqrt(ms + RMS_EPS)) * ng_ref[...]).astype(BF16)

    def proj(c0, c1):
        return _dot(xn, w_ref[:, c0:c1])

    def head_ms(u, gmat_ref):
        sq = u * u
        hi = sq.astype(BF16)
        lo = (sq - hi.astype(F32)).astype(BF16)
        return _dot(hi, gmat_ref[...]) + _dot(lo, gmat_ref[...])

    uq = proj(C_Q, C_K)
    qn = (uq * lax.rsqrt(head_ms(uq, gq_ref) + RMS_EPS)) * qg_ref[...]
    qT_ref[...] = (qn * (ATTN_SCALE * LOG2_E)).T.astype(BF16)

    uk = proj(C_K, C_V)
    kn_ref[...] = (uk * lax.rsqrt(head_ms(uk, gk_ref) + RMS_EPS)) * kg_ref[...]
    v_ref[...] = proj(C_V, C_AG)
    ag_ref[...] = proj(C_AG, C_IQ)
    iqT_ref[...] = proj(C_IQ, C_RQ).T.astype(BF16)
    ikw = proj(C_IKW, W_COLS)
    ikw_ref[...] = ikw
    kidx_ref[...] = ikw[:, 0:IDX_DIM]
    ikwT_ref[...] = ikw.T

    cosf = cos_ref[...]
    sinf = sin_ref[...]

    def rotary(u):
        parts = []
        for h in range(RET_HEADS):
            xh = u[:, h * RET_DK:(h + 1) * RET_DK]
            parts.append(xh * cosf + pltpu.roll(xh, RET_DK // 2, 1) * sinf)
        return jnp.concatenate(parts, axis=1)

    rq_ref[...] = rotary(proj(C_RQ, C_RK)).astype(BF16)
    rk_ref[...] = (rotary(proj(C_RK, C_RV)) * (RET_DK ** -0.5)).astype(BF16)
    rv_ref[...] = proj(C_RV, C_RG).astype(BF16)
    rg_ref[...] = proj(C_RG, C_IKW)


def _proj_call(x, ng, w, qg, kg, gq, gk, cosf, sinf, tm, key_bufs, buf_shape, layer, n_past):
    B, T, _ = x.shape
    nt = T // tm
    assert n_past % tm == 0
    row = lambda b, i: (b, i, 0)
    col = lambda b, i: (b, 0, i)
    const = lambda b, i: (0, 0)
    tok = lambda width: pl.BlockSpec((None, tm, width), row)
    tokT = lambda width: pl.BlockSpec((None, width, tm), col)
    buf = pl.BlockSpec((None, None, tm, LANES), lambda b, i: (layer, b, n_past // tm + i, 0))
    buf_sds = jax.ShapeDtypeStruct(buf_shape, F32)
    out_shape = (
        jax.ShapeDtypeStruct((B, ATTN_WIDTH, T), BF16),
        buf_sds,
        buf_sds,
        jax.ShapeDtypeStruct((B, T, ATTN_WIDTH), F32),
        jax.ShapeDtypeStruct((B, ATTN_WIDTH, T), BF16),
        buf_sds,
        jax.ShapeDtypeStruct((B, LANES, T), F32),
        jax.ShapeDtypeStruct((B, T, RET_WIDTH), BF16),
        jax.ShapeDtypeStruct((B, T, RET_WIDTH), BF16),
        jax.ShapeDtypeStruct((B, T, RET_WIDTH), BF16),
        jax.ShapeDtypeStruct((B, T, RET_WIDTH), F32),
        jax.ShapeDtypeStruct((buf_shape[0], B, T, IDX_DIM), F32),
    )
    kidx = pl.BlockSpec((None, None, tm, IDX_DIM), lambda b, i: (layer, b, i, 0))
    out_specs = (tokT(ATTN_WIDTH), buf, buf, tok(ATTN_WIDTH), tokT(ATTN_WIDTH),
                 buf, tokT(LANES), tok(RET_WIDTH), tok(RET_WIDTH), tok(RET_WIDTH), tok(RET_WIDTH), kidx)
    n_in = 9
    extra_in, extra_specs, aliases = (), [], {}
    if key_bufs is not None:
        extra_in = tuple(key_bufs)
        extra_specs = [pl.BlockSpec(memory_space=pl.ANY)] * 4
        aliases = {n_in: 1, n_in + 1: 2, n_in + 2: 5, n_in + 3: 11}
    in_specs = [
        tok(D_MODEL),
        pl.BlockSpec((1, D_MODEL), const),
        pl.BlockSpec((None, D_MODEL, W_COLS), lambda b, i: (layer, 0, 0)),
        pl.BlockSpec((1, ATTN_WIDTH), const),
        pl.BlockSpec((1, KV_WIDTH), const),
        pl.BlockSpec((ATTN_WIDTH, ATTN_WIDTH), const),
        pl.BlockSpec((KV_WIDTH, KV_WIDTH), const),
        pl.BlockSpec((tm, LANES), lambda b, i: (i, 0)),
        pl.BlockSpec((tm, LANES), lambda b, i: (i, 0)),
    ]
    return pl.pallas_call(
        _proj_kernel, grid=(B, nt), in_specs=in_specs + extra_specs, out_specs=out_specs,
        out_shape=out_shape, input_output_aliases=aliases,
        compiler_params=pltpu.CompilerParams(dimension_semantics=("parallel", "parallel"),
                                             vmem_limit_bytes=VMEM_LIMIT),
        name="proj",
    )(x, ng, w, qg, kg, gq, gk, cosf, sinf, *extra_in)


def _loop_pairs(n, body, init):
    def quad(i, carry):
        for r in range(4):
            carry = body(4 * i + r, carry)
        return carry

    def pair(i, carry):
        return body(2 * i + 1, body(2 * i, carry))

    carry = lax.fori_loop(0, n >> 2, quad, init)
    carry = lax.fori_loop((n >> 2) << 1, n >> 1, pair, carry)
    return lax.fori_loop(n & ~1, n, body, carry)


def _attn_kernel(iqT_ref, qT_ref, ikwT_ref, ikw_ref, kn_ref, v_ref, tri_ref, out_ref,
                 ikw16_ref, kn16_ref, vaug_ref, sc_ref, hi_ref, lo_ref, bias_ref, lg_ref, acc_ref,
                 *, pos0, ksel, nkb_max, lq):
    q0 = pos0 + pl.program_id(1) * lq
    lane = lax.broadcasted_iota(jnp.int32, (1, lq), 1)
    limit = (((q0 + lane) >> 6) + 1) << 6
    last_limit = (((q0 + lq - 1) >> 6) + 1) << 6
    nkb = jnp.minimum((last_limit + KEY_BLOCK - 1) // KEY_BLOCK, nkb_max)
    sub = lax.broadcasted_iota(jnp.int32, (KEY_BLOCK, 1), 0)

    @pl.when(pl.program_id(1) == 0)
    def _():
        ones = jnp.ones((DEN_ROWS, KEY_BLOCK), F32)

        def stage_body(kb, carry):
            off = pl.multiple_of(kb * KEY_BLOCK, KEY_BLOCK)
            ikw16_ref[pl.ds(off, KEY_BLOCK), :] = ikw_ref[pl.ds(off, KEY_BLOCK), :].astype(BF16)
            kn16_ref[pl.ds(off, KEY_BLOCK), :] = kn_ref[pl.ds(off, KEY_BLOCK), :].astype(BF16)
            vT = v_ref[pl.ds(off, KEY_BLOCK), :].T
            for g in range(N_KV_HEADS):
                vaug_ref[g, :, pl.ds(off, KEY_BLOCK)] = jnp.concatenate(
                    [vT[g * HEAD_DIM:(g + 1) * HEAD_DIM, :], ones], axis=0).astype(BF16)
            return carry

        lax.fori_loop(0, nkb_max, stage_body, 0)

    iqT = iqT_ref[...]
    qT = qT_ref[...]
    zpad = jnp.zeros((HEAD_DIM, lq), BF16)
    rhs_idx = [jnp.concatenate([iqT[h * IDX_DIM:(h + 1) * IDX_DIM, :], zpad], axis=0)
               for h in range(IDX_HEADS)]
    wT = ikwT_ref[...]
    wrows = [wT[IDX_DIM + h:IDX_DIM + h + 1, :] * (IDX_SCALE * IDX_HEAD_SCALE) for h in range(IDX_HEADS)]

    hp = max(1, MXU_WIDTH // lq)
    side_by_side = lambda ops: [jnp.concatenate(ops[i:i + hp], axis=1) for i in range(0, len(ops), hp)]

    def dot_heads(lhs, rhs_wide):
        res = _dot(lhs, rhs_wide)
        return [res[:, j * lq:(j + 1) * lq] for j in range(hp)]

    rhs_idx = side_by_side(rhs_idx)

    def score_body(kb, carry):
        off = pl.multiple_of(kb * KEY_BLOCK, KEY_BLOCK)
        kk = ikw16_ref[pl.ds(off, KEY_BLOCK), :]
        acc = None
        for i, rhs in enumerate(rhs_idx):
            for j, s in enumerate(dot_heads(kk, rhs)):
                term = jnp.maximum(s, 0.0) * wrows[i * hp + j]
                acc = term if acc is None else acc + term
        bits = pltpu.bitcast(acc, jnp.int32)
        key = bits ^ ((bits >> 31) & 0x7FFFFFFF)
        key = jnp.where(key == -1, 0, key)
        key = jnp.where(off + sub < limit, key, INT_MIN)
        sc_ref[pl.ds(off, KEY_BLOCK), :] = key
        hi_ref[pl.ds(off, KEY_BLOCK), :] = (key >> 16).astype(jnp.int16)
        lo_ref[pl.ds(off, KEY_BLOCK), :] = ((key & 0xFFFF) + I16_MIN).astype(jnp.int16)
        return carry

    _loop_pairs(nkb, score_body, 0)

    kf = float(ksel)

    def count16(ref, pred_fn):
        def body(kb, c):
            off = pl.multiple_of(kb * KEY_BLOCK, KEY_BLOCK)
            hit = jnp.where(pred_fn(ref[pl.ds(off, KEY_BLOCK), :]), jnp.int16(1), jnp.int16(0))
            parts = [hit[r:r + I16_ROWS, :] for r in range(0, KEY_BLOCK, I16_ROWS)]
            while len(parts) > 1:
                parts = [parts[r] + parts[r + 1] for r in range(0, len(parts), 2)]
            return c + parts[0]
        part = _loop_pairs(nkb, body, jnp.zeros((I16_ROWS, lq), jnp.int16))
        return jnp.sum(part.astype(F32), axis=0, keepdims=True)

    def kth_largest16(ref, k_row):
        def bit_body(i, t):
            cand = t + jnp.left_shift(jnp.int32(1), 15 - i)
            cand16 = cand.astype(jnp.int16)
            c = count16(ref, lambda blk: blk >= cand16)
            return jnp.where(c >= k_row, cand, t)
        return lax.fori_loop(0, 16, bit_body, jnp.full((1, lq), I16_MIN, jnp.int32))

    t_hi = kth_largest16(hi_ref, kf)
    t_hi16 = t_hi.astype(jnp.int16)
    k_lo = kf - count16(hi_ref, lambda blk: blk > t_hi16)

    def plane_body(kb, carry):
        off = pl.multiple_of(kb * KEY_BLOCK, KEY_BLOCK)
        lo_ref[pl.ds(off, KEY_BLOCK), :] = jnp.where(
            hi_ref[pl.ds(off, KEY_BLOCK), :] == t_hi16, lo_ref[pl.ds(off, KEY_BLOCK), :], jnp.int16(I16_MIN))
        return carry

    lax.fori_loop(0, nkb, plane_body, 0)
    t_lo = kth_largest16(lo_ref, k_lo)
    t_lo16 = t_lo.astype(jnp.int16)
    thr = (t_hi << 16) | (t_lo - I16_MIN)
    c_gt_lo = count16(lo_ref, lambda blk: blk > t_lo16)
    need = jnp.where(thr == INT_MIN, 0.0, k_lo - c_gt_lo)

    def qpad(h):
        qh = qT[h * HEAD_DIM:(h + 1) * HEAD_DIM, :]
        return jnp.concatenate([qh, zpad] if h // KV_GROUP == 0 else [zpad, qh], axis=0)

    rhs_q = side_by_side([qpad(h) for h in range(N_HEADS)])
    groups_per_kv = KV_GROUP // hp

    acc_ref[...] = jnp.zeros_like(acc_ref)
    max_init = tuple(jnp.full((SUBLANES, lq), NEG_INF, F32) for _ in range(KV_GROUP))

    def logits_group(g, off, bias, maxes):
        kb16 = kn16_ref[pl.ds(off, KEY_BLOCK), :]
        new = []
        for i in range(groups_per_kv):
            for j, raw in enumerate(dot_heads(kb16, rhs_q[g * groups_per_kv + i])):
                hl = i * hp + j
                lg = raw + bias
                lg_ref[g * KV_GROUP + hl, pl.ds(off, KEY_BLOCK), :] = lg
                new.append(jnp.maximum(maxes[hl], jnp.max(lg.reshape(KEY_BLOCK // SUBLANES, SUBLANES, lq), axis=0)))
        return tuple(new)

    def pv_group(g, off, ms):
        vaug = vaug_ref[g, :, pl.ds(off, KEY_BLOCK)]
        for i in range(groups_per_kv):
            hls = range(i * hp, (i + 1) * hp)
            p = [jnp.exp2(lg_ref[g * KV_GROUP + hl, pl.ds(off, KEY_BLOCK), :] - ms[hl]).astype(BF16) for hl in hls]
            for hl, upd in zip(hls, dot_heads(vaug, jnp.concatenate(p, axis=1))):
                acc_ref[g * KV_GROUP + hl] += upd

    def body_a(kb, carry):
        run_eq = carry[0]
        off = pl.multiple_of(kb * KEY_BLOCK, KEY_BLOCK)
        blk = sc_ref[pl.ds(off, KEY_BLOCK), :]
        eqf = jnp.where(blk == thr, 1.0, 0.0)
        rank = run_eq + _dot(tri_ref[...], eqf.astype(BF16))
        sel = jnp.where(blk > thr, 1.0, jnp.where(rank < need, eqf, 0.0))
        bias = (1.0 - sel) * NEG_INF
        bias_ref[pl.ds(off, KEY_BLOCK), :] = bias
        run_eq = run_eq + jnp.sum(eqf, axis=0, keepdims=True)
        return (run_eq,) + logits_group(0, off, bias, carry[1:])

    carry = _loop_pairs(nkb, body_a, (jnp.zeros((1, lq), F32),) + max_init)
    ms0 = [jnp.max(c, axis=0, keepdims=True) for c in carry[1:]]

    def body_b(kb, maxes):
        off = pl.multiple_of(kb * KEY_BLOCK, KEY_BLOCK)
        new = logits_group(1, off, bias_ref[pl.ds(off, KEY_BLOCK), :], maxes)
        pv_group(0, off, ms0)
        return new

    ms1 = [jnp.max(c, axis=0, keepdims=True) for c in _loop_pairs(nkb, body_b, max_init)]

    def body_c(kb, carry):
        pv_group(1, pl.multiple_of(kb * KEY_BLOCK, KEY_BLOCK), ms1)
        return carry

    _loop_pairs(nkb, body_c, 0)

    outs = []
    for h in range(N_HEADS):
        a = acc_ref[h]
        outs.append(a[0:HEAD_DIM, :] * (1.0 / a[HEAD_DIM:HEAD_DIM + 1, :]))
    out_ref[...] = jnp.concatenate(outs, axis=0).T


def _attn_call(iqT, qT, ikwT, ikw_keys, kn_keys, v_keys, tri, layer, pos0, t_real, ksel, lq):
    B, _, T = qT.shape
    Lk = ikw_keys.shape[2]
    qblk = lambda rows: pl.BlockSpec((None, rows, lq), lambda b, j: (b, 0, j))
    keys = pl.BlockSpec((None, None, Lk, LANES), lambda b, j: (layer, b, 0, 0))
    kern = functools.partial(_attn_kernel, pos0=pos0, ksel=ksel, nkb_max=Lk // KEY_BLOCK, lq=lq)
    return pl.pallas_call(
        kern, grid=(B, -(-t_real // lq)),
        in_specs=[qblk(ATTN_WIDTH), qblk(ATTN_WIDTH), qblk(LANES), keys, keys, keys,
                  pl.BlockSpec((KEY_BLOCK, KEY_BLOCK), lambda b, j: (0, 0))],
        out_specs=pl.BlockSpec((None, lq, ATTN_WIDTH), lambda b, j: (b, j, 0)),
        out_shape=jax.ShapeDtypeStruct((B, T, ATTN_WIDTH), F32),
        scratch_shapes=[pltpu.VMEM((Lk, LANES), BF16),
                        pltpu.VMEM((Lk, LANES), BF16),
                        pltpu.VMEM((N_KV_HEADS, HEAD_DIM + DEN_ROWS, Lk), BF16),
                        pltpu.VMEM((Lk, lq), jnp.int32),
                        pltpu.VMEM((Lk, lq), jnp.int16),
                        pltpu.VMEM((Lk, lq), jnp.int16),
                        pltpu.VMEM((Lk, lq), F32),
                        pltpu.VMEM((N_HEADS, Lk, lq), F32),
                        pltpu.VMEM((N_HEADS, HEAD_DIM + DEN_ROWS, lq), F32)],
        compiler_params=pltpu.CompilerParams(dimension_semantics=("arbitrary", "arbitrary"),
                                             vmem_limit_bytes=VMEM_LIMIT),
        name="attn",
    )(iqT, qT, ikwT, ikw_keys, kn_keys, v_keys, tri)


def _ret_kernel(rq_ref, rk_ref, rv_ref, s0_ref, dmask_ref, zeta_ref, xi_ref, gch_ref, g_ref,
                o_ref, sfin_ref, st_ref, *, n_last):
    n = pl.program_id(1)

    @pl.when(n == 0)
    def _():
        st_ref[...] = s0_ref[...]

    pairs = [(b, h) for b in range(rq_ref.shape[0]) for h in range(RET_HEADS)]
    sl = lambda h: slice(h * RET_DK, (h + 1) * RET_DK)
    inner = [(_dot_nt(rq_ref[b, :, sl(h)], rk_ref[b, :, sl(h)]) * dmask_ref[h]).astype(BF16) for b, h in pairs]
    cross = [_dot(rq_ref[b, :, sl(h)], st_ref[b, h].astype(BF16)) * xi_ref[h] for b, h in pairs]
    upd = [_dot((rk_ref[b, :, sl(h)].astype(F32) * zeta_ref[h]).T.astype(BF16), rv_ref[b, :, sl(h)])
           for b, h in pairs]
    for i, (b, h) in enumerate(pairs):
        o = _dot(inner[i], rv_ref[b, :, sl(h)]) + cross[i]
        st_ref[b, h] = gch_ref[h] * st_ref[b, h] + upd[i]
        ms = jnp.mean(o * o, axis=-1, keepdims=True)
        o_ref[b, :, sl(h)] = (o * lax.rsqrt(ms + RMS_EPS)) * g_ref[:, sl(h)]

    @pl.when(n == n_last)
    def _():
        sfin_ref[...] = st_ref[...]


def _ret_tables(c):
    log_g = jnp.log(1.0 - 2.0 ** (-5.0 - jnp.arange(RET_HEADS, dtype=F32)))
    i = jnp.arange(c, dtype=F32)
    diff = i[:, None] - i[None, :]
    dmask = jnp.where(diff[None] >= 0, jnp.exp(log_g[:, None, None] * jnp.maximum(diff, 0.0)[None]), 0.0)
    zeta = jnp.exp(log_g[:, None] * (c - 1 - i)[None, :])
    xi = jnp.exp(log_g[:, None] * (i + 1)[None, :])
    gch = jnp.exp(log_g * c)
    bc = lambda t: jnp.broadcast_to(t[:, :, None], (RET_HEADS, c, LANES))
    return dmask, bc(zeta), bc(xi), jnp.broadcast_to(gch[:, None, None], (RET_HEADS, 1, LANES))


def _ret_call(rq, rk, rv, s0, ret_g, t_real, c):
    B, T, _ = rq.shape
    nc = T // c
    dmask, zeta, xi, gch = _ret_tables(c)
    bb = RET_BATCH_ROWS if B % RET_BATCH_ROWS == 0 else 1
    tok =pl.BlockSpec((bb, c, RET_WIDTH), lambda b, n: (b, n, 0))
    st = pl.BlockSpec((bb, RET_HEADS, RET_DK, RET_DV), lambda b, n: (b, 0, 0, 0))
    tab = lambda s: pl.BlockSpec(s, lambda b, n: (0,) * len(s))
    return pl.pallas_call(
        functools.partial(_ret_kernel, n_last=t_real // c - 1), grid=(B // bb, nc),
        in_specs=[tok, tok, tok, st, tab((RET_HEADS, c, c)), tab((RET_HEADS, c, LANES)),
                  tab((RET_HEADS, c, LANES)), tab((RET_HEADS, 1, LANES)), tab((1, RET_WIDTH))],
        out_specs=(tok, st),
        out_shape=(jax.ShapeDtypeStruct((B, T, RET_WIDTH), F32),
                   jax.ShapeDtypeStruct((B, RET_HEADS, RET_DK, RET_DV), F32)),
        scratch_shapes=[pltpu.VMEM((bb, RET_HEADS, RET_DK, RET_DV), F32)],
        compiler_params=pltpu.CompilerParams(dimension_semantics=("parallel", "arbitrary"),
                                             vmem_limit_bytes=VMEM_LIMIT),
        name="ret",
    )(rq, rk, rv, s0, dmask, zeta, xi, gch, ret_g)


def _out_kernel(h_ref, attn_ref, ag_ref, ret_ref, rg_ref, p_ref, wo_ref, pg_ref, pp_ref, png_ref, o_ref):
    ag = ag_ref[...]
    rg = rg_ref[...]
    mix_a = (attn_ref[...] * (ag * jax.nn.sigmoid(ag))).astype(BF16)
    mix_r = (ret_ref[...] * (rg * jax.nn.sigmoid(rg))).astype(BF16)
    h1 = h_ref[...] + _dot(mix_a, wo_ref[0:ATTN_WIDTH, :]) + _dot(mix_r, wo_ref[ATTN_WIDTH:, :])
    ms = jnp.mean(h1 * h1, axis=-1, keepdims=True)
    hn = ((h1 * lax.rsqrt(ms + RMS_EPS)) * png_ref[...]).astype(BF16)
    gate = jax.nn.sigmoid(_dot(hn, pg_ref[...]))
    o_ref[...] = h1 + gate * _dot(p_ref[...].astype(BF16), pp_ref[...])


def _out_call(h, attn, ag, ret, rg, p, wo, pgate, pproj, png, tm, layer):
    B, T, _ = h.shape
    tok = lambda width: pl.BlockSpec((None, tm, width), lambda b, i: (b, i, 0))
    stacked = lambda s: pl.BlockSpec((None,) + s, lambda b, i: (layer, 0, 0))
    return pl.pallas_call(
        _out_kernel, grid=(B, T // tm),
        in_specs=[tok(D_MODEL), tok(ATTN_WIDTH), tok(ATTN_WIDTH), tok(RET_WIDTH), tok(RET_WIDTH),
                  pl.BlockSpec((None, None, tm, PLE_DIM), lambda b, i: (layer, b, i, 0)),
                  stacked((D_MODEL, D_MODEL)), stacked((D_MODEL, D_MODEL)), stacked((PLE_DIM, D_MODEL)),
                  pl.BlockSpec((1, D_MODEL), lambda b, i: (0, 0))],
        out_specs=tok(D_MODEL),
        out_shape=jax.ShapeDtypeStruct((B, T, D_MODEL), F32),
        compiler_params=pltpu.CompilerParams(dimension_semantics=("parallel", "parallel"),
                                             vmem_limit_bytes=VMEM_LIMIT),
        name="out",
    )(h, attn, ag, ret, rg, p, wo, pgate, pproj, png)


def _rope_tables(pos0, t):
    half = RET_DK // 2
    inv = ROPE_BASE ** (-jnp.arange(half, dtype=F32) / half)
    ang = (pos0 + jnp.arange(t, dtype=jnp.int32)).astype(F32)[:, None] * inv[None, :]
    cos, sin = jnp.cos(ang), jnp.sin(ang)
    return jnp.concatenate([cos, cos], axis=1), jnp.concatenate([-sin, sin], axis=1)


def _group_mean_matrix(width, group):
    i = jnp.arange(width) // group
    return jnp.where(i[:, None] == i[None, :], 1.0 / group, 0.0).astype(BF16)


def _strict_lower(n):
    return (jnp.arange(n)[None, :] < jnp.arange(n)[:, None]).astype(BF16)


def _key_rows(n_past, t):
    return -(-(n_past + t) // KEY_BLOCK) * KEY_BLOCK


def _layer(h, p, layer, depth, n_past, t_real, key_bufs, s0, lw, consts, tm, chunk, lq):
    w, wo, ng, qg, kg, ret_g, pproj, pgate, png = lw
    gq, gk, tri = consts
    B, T, _ = h.shape
    cosf, sinf = _rope_tables(n_past, T)
    buf_shape = (depth, B, _key_rows(n_past, T), LANES)
    qT, kn_buf, v_buf, ag, iqT, ikw_buf, ikwT, rq, rk, rv, rg, kidx_buf = _proj_call(
        h, ng, w, qg, kg, gq, gk, cosf, sinf, tm, key_bufs, buf_shape, layer, n_past)
    ksel = min(TOPK_MAX, (n_past + t_real) // 4)
    attn = _attn_call(iqT, qT, ikwT, ikw_buf, kn_buf, v_buf, tri, layer, n_past, t_real, ksel, lq)
    ret, s_new = _ret_call(rq, rk, rv, s0, ret_g, t_real, chunk)
    h_new = _out_call(h, attn, ag, ret, rg, p, wo, pgate, pproj, png, tm, layer)
    return h_new, (kn_buf, v_buf, ikw_buf, kidx_buf), s_new


def _permute_w_in(w_in):
    depth = w_in.shape[0]
    a = w_in[:, :, 0:1280]
    iq = w_in[:, :, 1280:1792]
    ik_iw = w_in[:, :, 1792:1864]
    r = w_in[:, :, 1864:3912]
    pad = jnp.zeros((depth, D_MODEL, W_COLS - C_IKW - 72), w_in.dtype)
    return jnp.concatenate([a, iq, r, ik_iw, pad], axis=2).astype(BF16)


def kernel(x_prompt, x_sample, cache_k, cache_v, cache_k_idx, state_ret, p_prompt, p_sample,
           w_in, w_out, norm_g, q_norm_g, k_norm_g, ret_norm_g, ple_proj, ple_gate, ple_norm_g):
    depth = w_in.shape[0]
    b_p, t_p, _ = x_prompt.shape
    b_s, t_s, _ = x_sample.shape
    past = cache_k.shape[2]
    t_s_pad = -(-t_s // LANES) * LANES

    w_perm = _permute_w_in(w_in)
    wo = w_out.astype(BF16)
    pproj = ple_proj.astype(BF16)
    pgate = ple_gate.astype(BF16)
    qg = jnp.tile(q_norm_g, (1, N_HEADS))[:, None, :]
    kg = jnp.tile(k_norm_g, (1, N_KV_HEADS))[:, None, :]
    consts = (_group_mean_matrix(ATTN_WIDTH, HEAD_DIM), _group_mean_matrix(KV_WIDTH, HEAD_DIM),
              _strict_lower(KEY_BLOCK))

    pad_t = lambda a, axis: jnp.pad(a, [(0, t_s_pad - t_s) if d == axis else (0, 0) for d in range(a.ndim)])
    h_p = x_prompt
    h_s = pad_t(x_sample, 1)
    p_s = pad_t(p_sample, 2)
    r0 = jnp.zeros((b_p, RET_HEADS, RET_DK, RET_DV), F32)
    grow = lambda a: jnp.pad(a, ((0, 0), (0, 0), (0, _key_rows(past, t_s_pad) - past), (0, LANES - a.shape[-1])))
    bufs_s = (grow(cache_k.reshape(depth, b_s, past, KV_WIDTH)), grow(cache_v.reshape(depth, b_s, past, KV_WIDTH)),
              grow(cache_k_idx), jnp.zeros((depth, b_s, t_s_pad, IDX_DIM), F32))
    bufs_p = None

    states_p, states_s = [], []
    for i in range(depth):
        lw = (w_perm, wo, norm_g[i][None], qg[i], kg[i], ret_norm_g[i][None], pproj, pgate, ple_norm_g[i][None])
        h_p, bufs_p, rp = _layer(h_p, p_prompt, i, depth, 0, t_p, bufs_p, r0, lw, consts,
                                 PROMPT_ROWS, 2 * CHUNK, 2 * LANES)
        h_s, bufs_s, rs = _layer(h_s, p_s, i, depth, past, t_s, bufs_s, state_ret[i], lw, consts,
                                 LANES, CHUNK, LANES)
        states_p.append(rp)
        states_s.append(rs)

    heads = lambda a: a.reshape(a.shape[:-1] + (N_KV_HEADS, HEAD_DIM))
    new_s = [b[:, :, past:past + t_s] for b in bufs_s[:2]]
    return (h_p, h_s[:, :t_s],
            heads(bufs_p[0]), heads(bufs_p[1]), bufs_p[3], jnp.stack(states_p),
            heads(new_s[0]), heads(new_s[1]), bufs_s[3][:, :, :t_s], jnp.stack(states_s))
```

```python
import functools

import jax
import jax.numpy as jnp
from jax import lax
from jax.experimental import pallas as pl
from jax.experimental.pallas import tpu as pltpu

F32 = jnp.float32
BF16 = jnp.bfloat16

D_MODEL = 1024
CHUNK = 64
ATTN_WIDTH = 512
RET_WIDTH = 512
N_HEADS = 8
HEAD_DIM = 64
N_KV_HEADS = 2
KV_GROUP = N_HEADS // N_KV_HEADS
KV_WIDTH = N_KV_HEADS * HEAD_DIM
IDX_HEADS = 8
IDX_DIM = 64
TOPK_MAX = 256
RET_HEADS = 4
RET_DK = 128
RET_DV = 128
PLE_DIM = 256
RMS_EPS = 1e-6
NEG_INF = -1e30
ATTN_SCALE = HEAD_DIM ** -0.5
IDX_SCALE = IDX_DIM ** -0.5
IDX_HEAD_SCALE = IDX_HEADS ** -0.5
ROPE_BASE = 10000.0
LOG2_E = 1.4426950408889634

LANES = 128
SUBLANES = 8
MXU_WIDTH = 256
KEY_BLOCK = MXU_WIDTH
PROMPT_ROWS = 512
RET_BATCH_ROWS = 4
DEN_ROWS = 16
INT_MIN = -(2 ** 31)
I16_MIN = -(2 ** 15)
I16_ROWS = 16
VMEM_LIMIT = 56 * 1024 * 1024

C_Q = 0
C_K = 512
C_V = 640
C_AG = 768
C_IQ = 1280
C_RQ = 1792
C_RK = 2304
C_RV = 2816
C_RG = 3328
C_IKW = 3840
W_COLS = 3968


def _dot(a, b):
    return jnp.dot(a, b, preferred_element_type=F32)


def _dot_nt(a, b):
    return lax.dot_general(a, b, (((1,), (1,)), ((), ())), preferred_element_type=F32)


N_PROJ_OUTS = 12


def _proj_kernel(x_ref, ng_ref, w_ref, qg_ref, kg_ref, gq_ref, gk_ref, cos_ref, sin_ref, *refs):
    (qT_ref, kn_ref, v_ref, ag_ref, iqT_ref, ikw_ref, ikwT_ref,
     rq_ref, rk_ref, rv_ref, rg_ref, kidx_ref) = refs[-N_PROJ_OUTS:]
    x = x_ref[...]
    ms = jnp.mean(x * x, axis=-1, keepdims=True)
    xn = ((x * lax.rsqrt(ms + RMS_EPS)) * ng_ref[...]).astype(BF16)

    def proj(c0, c1):
        return _dot(xn, w_ref[:, c0:c1])

    def head_ms(u, gmat_ref):
        sq = u * u
        hi = sq.astype(BF16)
        lo = (sq - hi.astype(F32)).astype(BF16)
        return _dot(hi, gmat_ref[...]) + _dot(lo, gmat_ref[...])

    uq = proj(C_Q, C_K)
    qn = (uq * lax.rsqrt(head_ms(uq, gq_ref) + RMS_EPS)) * qg_ref[...]
    qT_ref[...] = (qn * (ATTN_SCALE * LOG2_E)).T.astype(BF16)

    uk = proj(C_K, C_V)
    kn_ref[...] = (uk * lax.rsqrt(head_ms(uk, gk_ref) + RMS_EPS)) * kg_ref[...]
    v_ref[...] = proj(C_V, C_AG)
    ag_ref[...] = proj(C_AG, C_IQ)
    iqT_ref[...] = proj(C_IQ, C_RQ).T.astype(BF16)
    ikw = proj(C_IKW, W_COLS)
    ikw_ref[...] = ikw
    kidx_ref[...] = ikw[:, 0:IDX_DIM]
    ikwT_ref[...] = ikw.T

    cosf = cos_ref[...]
    sinf = sin_ref[...]

    def rotary(u):
        parts = []
        for h in range(RET_HEADS):
            xh = u[:, h * RET_DK:(h + 1) * RET_DK]
            parts.append(xh * cosf + pltpu.roll(xh, RET_DK // 2, 1) * sinf)
        return jnp.concatenate(parts, axis=1)

    rq_ref[...] = rotary(proj(C_RQ, C_RK)).astype(BF16)
    rk_ref[...] = (rotary(proj(C_RK, C_RV)) * (RET_DK ** -0.5)).astype(BF16)
    rv_ref[...] = proj(C_RV, C_RG).astype(BF16)
    rg_ref[...] = proj(C_RG, C_IKW)


def _proj_call(x, ng, w, qg, kg, gq, gk, cosf, sinf, tm, key_bufs, buf_shape, layer, n_past):
    B, T, _ = x.shape
    nt = T // tm
    assert n_past % tm == 0
    row = lambda b, i: (b, i, 0)
    col = lambda b, i: (b, 0, i)
    const = lambda b, i: (0, 0)
    tok = lambda width: pl.BlockSpec((None, tm, width), row)
    tokT = lambda width: pl.BlockSpec((None, width, tm), col)
    buf = pl.BlockSpec((None, None, tm, LANES), lambda b, i: (layer, b, n_past // tm + i, 0))
    buf_sds = jax.ShapeDtypeStruct(buf_shape, F32)
    out_shape = (
        jax.ShapeDtypeStruct((B, ATTN_WIDTH, T), BF16),
        buf_sds,
        buf_sds,
        jax.ShapeDtypeStruct((B, T, ATTN_WIDTH), F32),
        jax.ShapeDtypeStruct((B, ATTN_WIDTH, T), BF16),
        buf_sds,
        jax.ShapeDtypeStruct((B, LANES, T), F32),
        jax.ShapeDtypeStruct((B, T, RET_WIDTH), BF16),
        jax.ShapeDtypeStruct((B, T, RET_WIDTH), BF16),
        jax.ShapeDtypeStruct((B, T, RET_WIDTH), BF16),
        jax.ShapeDtypeStruct((B, T, RET_WIDTH), F32),
        jax.ShapeDtypeStruct((buf_shape[0], B, T, IDX_DIM), F32),
    )
    kidx = pl.BlockSpec((None, None, tm, IDX_DIM), lambda b, i: (layer, b, i, 0))
    out_specs = (tokT(ATTN_WIDTH), buf, buf, tok(ATTN_WIDTH), tokT(ATTN_WIDTH),
                 buf, tokT(LANES), tok(RET_WIDTH), tok(RET_WIDTH), tok(RET_WIDTH), tok(RET_WIDTH), kidx)
    n_in = 9
    extra_in, extra_specs, aliases = (), [], {}
    if key_bufs is not None:
        extra_in = tuple(key_bufs)
        extra_specs = [pl.BlockSpec(memory_space=pl.ANY)] * 4
        aliases = {n_in: 1, n_in + 1: 2, n_in + 2: 5, n_in + 3: 11}
    in_specs = [
        tok(D_MODEL),
        pl.BlockSpec((1, D_MODEL), const),
        pl.BlockSpec((None, D_MODEL, W_COLS), lambda b, i: (layer, 0, 0)),
        pl.BlockSpec((1, ATTN_WIDTH), const),
        pl.BlockSpec((1, KV_WIDTH), const),
        pl.BlockSpec((ATTN_WIDTH, ATTN_WIDTH), const),
        pl.BlockSpec((KV_WIDTH, KV_WIDTH), const),
        pl.BlockSpec((tm, LANES), lambda b, i: (i, 0)),
        pl.BlockSpec((tm, LANES), lambda b, i: (i, 0)),
    ]
    return pl.pallas_call(
        _proj_kernel, grid=(B, nt), in_specs=in_specs + extra_specs, out_specs=out_specs,
        out_shape=out_shape, input_output_aliases=aliases,
        compiler_params=pltpu.CompilerParams(dimension_semantics=("parallel", "parallel"),
                                             vmem_limit_bytes=VMEM_LIMIT),
        name="proj",
    )(x, ng, w, qg, kg, gq, gk, cosf, sinf, *extra_in)


def _loop_pairs(n, body, init):
    def quad(i, carry):
        for r in range(4):
            carry = body(4 * i + r, carry)
        return carry

    def pair(i, carry):
        return body(2 * i + 1, body(2 * i, carry))

    carry = lax.fori_loop(0, n >> 2, quad, init)
    carry = lax.fori_loop((n >> 2) << 1, n >> 1, pair, carry)
    return lax.fori_loop(n & ~1, n, body, carry)


def _attn_kernel(iqT_ref, qT_ref, ikwT_ref, ikw_ref, kn_ref, v_ref, tri_ref, out_ref,
                 ikw16_ref, kn16_ref, vaug_ref, sc_ref, hi_ref, lo_ref, bias_ref, lg_ref, acc_ref,
                 *, pos0, ksel, nkb_max, lq):
    q0 = pos0 + pl.program_id(1) * lq
    lane = lax.broadcasted_iota(jnp.int32, (1, lq), 1)
    limit = (((q0 + lane) >> 6) + 1) << 6
    last_limit = (((q0 + lq - 1) >> 6) + 1) << 6
    nkb = jnp.minimum((last_limit + KEY_BLOCK - 1) // KEY_BLOCK, nkb_max)
    sub = lax.broadcasted_iota(jnp.int32, (KEY_BLOCK, 1), 0)

    @pl.when(pl.program_id(1) == 0)
    def _():
        ones = jnp.ones((DEN_ROWS, KEY_BLOCK), F32)

        def stage_body(kb, carry):
            off = pl.multiple_of(kb * KEY_BLOCK, KEY_BLOCK)
            ikw16_ref[pl.ds(off, KEY_BLOCK), :] = ikw_ref[pl.ds(off, KEY_BLOCK), :].astype(BF16)
            kn16_ref[pl.ds(off, KEY_BLOCK), :] = kn_ref[pl.ds(off, KEY_BLOCK), :].astype(BF16)
            vT = v_ref[pl.ds(off, KEY_BLOCK), :].T
            for g in range(N_KV_HEADS):
                vaug_ref[g, :, pl.ds(off, KEY_BLOCK)] = jnp.concatenate(
                    [vT[g * HEAD_DIM:(g + 1) * HEAD_DIM, :], ones], axis=0).astype(BF16)
            return carry

        lax.fori_loop(0, nkb_max, stage_body, 0)

    iqT = iqT_ref[...]
    qT = qT_ref[...]
    zpad = jnp.zeros((HEAD_DIM, lq), BF16)
    rhs_idx = [jnp.concatenate([iqT[h * IDX_DIM:(h + 1) * IDX_DIM, :], zpad], axis=0)
               for h in range(IDX_HEADS)]
    wT = ikwT_ref[...]
    wrows = [wT[IDX_DIM + h:IDX_DIM + h + 1, :] * (IDX_SCALE * IDX_HEAD_SCALE) for h in range(IDX_HEADS)]

    hp = max(1, MXU_WIDTH // lq)
    side_by_side = lambda ops: [jnp.concatenate(ops[i:i + hp], axis=1) for i in range(0, len(ops), hp)]

    def dot_heads(lhs, rhs_wide):
        res = _dot(lhs, rhs_wide)
        return [res[:, j * lq:(j + 1) * lq] for j in range(hp)]

    rhs_idx = side_by_side(rhs_idx)

    def score_body(kb, carry):
        off = pl.multiple_of(kb * KEY_BLOCK, KEY_BLOCK)
        kk = ikw16_ref[pl.ds(off, KEY_BLOCK), :]
        acc = None
        for i, rhs in enumerate(rhs_idx):
            for j, s in enumerate(dot_heads(kk, rhs)):
                term = jnp.maximum(s, 0.0) * wrows[i * hp + j]
                acc = term if acc is None else acc + term
        bits = pltpu.bitcast(acc, jnp.int32)
        key = bits ^ ((bits >> 31) & 0x7FFFFFFF)
        key = jnp.where(key == -1, 0, key)
        key = jnp.where(off + sub < limit, key, INT_MIN)
        sc_ref[pl.ds(off, KEY_BLOCK), :] = key
        hi_ref[pl.ds(off, KEY_BLOCK), :] = (key >> 16).astype(jnp.int16)
        lo_ref[pl.ds(off, KEY_BLOCK), :] = ((key & 0xFFFF) + I16_MIN).astype(jnp.int16)
        return carry

    _loop_pairs(nkb, score_body, 0)

    kf = float(ksel)

    def count16(ref, pred_fn):
        def body(kb, c):
            off = pl.multiple_of(kb * KEY_BLOCK, KEY_BLOCK)
            hit = jnp.where(pred_fn(ref[pl.ds(off, KEY_BLOCK), :]), jnp.int16(1), jnp.int16(0))
            parts = [hit[r:r + I16_ROWS, :] for r in range(0, KEY_BLOCK, I16_ROWS)]
            while len(parts) > 1:
                parts = [parts[r] + parts[r + 1] for r in range(0, len(parts), 2)]
            return c + parts[0]
        part = _loop_pairs(nkb, body, jnp.zeros((I16_ROWS, lq), jnp.int16))
        return jnp.sum(part.astype(F32), axis=0, keepdims=True)

    def kth_largest16(ref, k_row):
        def bit_body(i, t):
            cand = t + jnp.left_shift(jnp.int32(1), 15 - i)
            cand16 = cand.astype(jnp.int16)
            c = count16(ref, lambda blk: blk >= cand16)
            return jnp.where(c >= k_row, cand, t)
        return lax.fori_loop(0, 16, bit_body, jnp.full((1, lq), I16_MIN, jnp.int32))

    t_hi = kth_largest16(hi_ref, kf)
    t_hi16 = t_hi.astype(jnp.int16)
    k_lo = kf - count16(hi_ref, lambda blk: blk > t_hi16)

    def plane_body(kb, carry):
        off = pl.multiple_of(kb * KEY_BLOCK, KEY_BLOCK)
        lo_ref[pl.ds(off, KEY_BLOCK), :] = jnp.where(
            hi_ref[pl.ds(off, KEY_BLOCK), :] == t_hi16, lo_ref[pl.ds(off, KEY_BLOCK), :], jnp.int16(I16_MIN))
        return carry

    lax.fori_loop(0, nkb, plane_body, 0)
    t_lo = kth_largest16(lo_ref, k_lo)
    t_lo16 = t_lo.astype(jnp.int16)
    thr = (t_hi << 16) | (t_lo - I16_MIN)
    c_gt_lo = count16(lo_ref, lambda blk: blk > t_lo16)
    need = jnp.where(thr == INT_MIN, 0.0, k_lo - c_gt_lo)
    c_eq = count16(lo_ref, lambda blk: blk >= t_lo16) - c_gt_lo
    all_ties = (need > 0.0) & (need >= c_eq)
    some_ties = (need > 0.0) & (need < c_eq)
    thr_open = jnp.where(all_ties, thr - 1, thr)
    ranks_needed = jnp.max(jnp.where(some_ties, 1.0, 0.0)) > 0.0

    def qpad(h):
        qh = qT[h * HEAD_DIM:(h + 1) * HEAD_DIM, :]
        return jnp.concatenate([qh, zpad] if h // KV_GROUP == 0 else [zpad, qh], axis=0)

    rhs_q = side_by_side([qpad(h) for h in range(N_HEADS)])
    groups_per_kv = KV_GROUP // hp

    acc_ref[...] = jnp.zeros_like(acc_ref)
    max_init = tuple(jnp.full((SUBLANES, lq), NEG_INF, F32) for _ in range(KV_GROUP))

    def logits_group(g, off, bias, maxes):
        kb16 = kn16_ref[pl.ds(off, KEY_BLOCK), :]
        new = []
        for i in range(groups_per_kv):
            for j, raw in enumerate(dot_heads(kb16, rhs_q[g * groups_per_kv + i])):
                hl = i * hp + j
                lg = raw + bias
                lg_ref[g * KV_GROUP + hl, pl.ds(off, KEY_BLOCK), :] = lg
                new.append(jnp.maximum(maxes[hl], jnp.max(lg.reshape(KEY_BLOCK // SUBLANES, SUBLANES, lq), axis=0)))
        return tuple(new)

    def pv_group(g, off, ms):
        vaug = vaug_ref[g, :, pl.ds(off, KEY_BLOCK)]
        for i in range(groups_per_kv):
            hls = range(i * hp, (i + 1) * hp)
            p = [jnp.exp2(lg_ref[g * KV_GROUP + hl, pl.ds(off, KEY_BLOCK), :] - ms[hl]).astype(BF16) for hl in hls]
            for hl, upd in zip(hls, dot_heads(vaug, jnp.concatenate(p, axis=1))):
                acc_ref[g * KV_GROUP + hl] += upd

    def body_a(kb, carry):
        run_eq = carry[0]
        off = pl.multiple_of(kb * KEY_BLOCK, KEY_BLOCK)
        blk = sc_ref[pl.ds(off, KEY_BLOCK), :]
        eqf = jnp.where(blk == thr, 1.0, 0.0)
        rank = run_eq + _dot(tri_ref[...], eqf.astype(BF16))
        sel = jnp.where(blk > thr, 1.0, jnp.where(rank < need, eqf, 0.0))
        bias = (1.0 - sel) * NEG_INF
        bias_ref[pl.ds(off, KEY_BLOCK), :] = bias
        run_eq = run_eq + jnp.sum(eqf, axis=0, keepdims=True)
        return (run_eq,) + logits_group(0, off, bias, carry[1:])

    def body_a_no_ranks(kb, maxes):
        off = pl.multiple_of(kb * KEY_BLOCK, KEY_BLOCK)
        bias = jnp.where(sc_ref[pl.ds(off, KEY_BLOCK), :] > thr_open, 0.0, NEG_INF)
        bias_ref[pl.ds(off, KEY_BLOCK), :] = bias
        return logits_group(0, off, bias, maxes)

    maxes0 = lax.cond(
        ranks_needed,
        lambda: _loop_pairs(nkb, body_a, (jnp.zeros((1, lq), F32),) + max_init)[1:],
        lambda: _loop_pairs(nkb, body_a_no_ranks, max_init))
    ms0 = [jnp.max(c, axis=0, keepdims=True) for c in maxes0]

    def body_b(kb, maxes):
        off = pl.multiple_of(kb * KEY_BLOCK, KEY_BLOCK)
        new = logits_group(1, off, bias_ref[pl.ds(off, KEY_BLOCK), :], maxes)
        pv_group(0, off, ms0)
        return new

    ms1 = [jnp.max(c, axis=0, keepdims=True) for c in _loop_pairs(nkb, body_b, max_init)]

    def body_c(kb, carry):
        pv_group(1, pl.multiple_of(kb * KEY_BLOCK, KEY_BLOCK), ms1)
        return carry

    _loop_pairs(nkb, body_c, 0)

    outs = []
    for h in range(N_HEADS):
        a = acc_ref[h]
        outs.append(a[0:HEAD_DIM, :] * (1.0 / a[HEAD_DIM:HEAD_DIM + 1, :]))
    out_ref[...] = jnp.concatenate(outs, axis=0).T


def _attn_call(iqT, qT, ikwT, ikw_keys, kn_keys, v_keys, tri, layer, pos0, t_real, ksel, lq):
    B, _, T = qT.shape
    Lk = ikw_keys.shape[2]
    qblk = lambda rows: pl.BlockSpec((None, rows, lq), lambda b, j: (b, 0, j))
    keys = pl.BlockSpec((None, None, Lk, LANES), lambda b, j: (layer, b, 0, 0))
    kern = functools.partial(_attn_kernel, pos0=pos0, ksel=ksel, nkb_max=Lk // KEY_BLOCK, lq=lq)
    return pl.pallas_call(
        kern, grid=(B, -(-t_real // lq)),
        in_specs=[qblk(ATTN_WIDTH), qblk(ATTN_WIDTH), qblk(LANES), keys, keys, keys,
                  pl.BlockSpec((KEY_BLOCK, KEY_BLOCK), lambda b, j: (0, 0))],
        out_specs=pl.BlockSpec((None, lq, ATTN_WIDTH), lambda b, j: (b, j, 0)),
        out_shape=jax.ShapeDtypeStruct((B, T, ATTN_WIDTH), F32),
        scratch_shapes=[pltpu.VMEM((Lk, LANES), BF16),
                        pltpu.VMEM((Lk, LANES), BF16),
                        pltpu.VMEM((N_KV_HEADS, HEAD_DIM + DEN_ROWS, Lk), BF16),
                        pltpu.VMEM((Lk, lq), jnp.int32),
                        pltpu.VMEM((Lk, lq), jnp.int16),
                        pltpu.VMEM((Lk, lq), jnp.int16),
                        pltpu.VMEM((Lk, lq), F32),
                        pltpu.VMEM((N_HEADS, Lk, lq), F32),
                        pltpu.VMEM((N_HEADS, HEAD_DIM + DEN_ROWS, lq), F32)],
        compiler_params=pltpu.CompilerParams(dimension_semantics=("arbitrary", "arbitrary"),
                                             vmem_limit_bytes=VMEM_LIMIT),
        name="attn",
    )(iqT, qT, ikwT, ikw_keys, kn_keys, v_keys, tri)


def _ret_kernel(rq_ref, rk_ref, rv_ref, s0_ref, dmask_ref, zeta_ref, xi_ref, gch_ref, g_ref,
                o_ref, sfin_ref, st_ref, *, n_last):
    n = pl.program_id(1)

    @pl.when(n == 0)
    def _():
        st_ref[...] = s0_ref[...]

    pairs = [(b, h) for b in range(rq_ref.shape[0]) for h in range(RET_HEADS)]
    sl = lambda h: slice(h * RET_DK, (h + 1) * RET_DK)
    inner = [(_dot_nt(rq_ref[b, :, sl(h)], rk_ref[b, :, sl(h)]) * dmask_ref[h]).astype(BF16) for b, h in pairs]
    cross = [_dot(rq_ref[b, :, sl(h)], st_ref[b, h].astype(BF16)) * xi_ref[h] for b, h in pairs]
    upd = [_dot((rk_ref[b, :, sl(h)].astype(F32) * zeta_ref[h]).T.astype(BF16), rv_ref[b, :, sl(h)])
           for b, h in pairs]
    for i, (b, h) in enumerate(pairs):
        o = _dot(inner[i], rv_ref[b, :, sl(h)]) + cross[i]
        st_ref[b, h] = gch_ref[h] * st_ref[b, h] + upd[i]
        ms = jnp.mean(o * o, axis=-1, keepdims=True)
        o_ref[b, :, sl(h)] = (o * lax.rsqrt(ms + RMS_EPS)) * g_ref[:, sl(h)]

    @pl.when(n == n_last)
    def _():
        sfin_ref[...] = st_ref[...]


def _ret_tables(c):
    log_g = jnp.log(1.0 - 2.0 ** (-5.0 - jnp.arange(RET_HEADS, dtype=F32)))
    i = jnp.arange(c, dtype=F32)
    diff = i[:, None] - i[None, :]
    dmask = jnp.where(diff[None] >= 0, jnp.exp(log_g[:, None, None] * jnp.maximum(diff, 0.0)[None]), 0.0)
    zeta = jnp.exp(log_g[:, None] * (c - 1 - i)[None, :])
    xi = jnp.exp(log_g[:, None] * (i + 1)[None, :])
    gch = jnp.exp(log_g * c)
    bc = lambda t: jnp.broadcast_to(t[:, :, None], (RET_HEADS, c, LANES))
    return dmask, bc(zeta), bc(xi), jnp.broadcast_to(gch[:, None, None], (RET_HEADS, 1, LANES))


def _ret_call(rq, rk, rv, s0, ret_g, t_real, c):
    B, T, _ = rq.shape
    nc = T // c
    dmask, zeta, xi, gch = _ret_tables(c)
    bb = RET_BATCH_ROWS if B % RET_BATCH_ROWS == 0 else 1
    tok =pl.BlockSpec((bb, c, RET_WIDTH), lambda b, n: (b, n, 0))
    st = pl.BlockSpec((bb, RET_HEADS, RET_DK, RET_DV), lambda b, n: (b, 0, 0, 0))
    tab = lambda s: pl.BlockSpec(s, lambda b, n: (0,) * len(s))
    return pl.pallas_call(
        functools.partial(_ret_kernel, n_last=t_real // c - 1), grid=(B // bb, nc),
        in_specs=[tok, tok, tok, st, tab((RET_HEADS, c, c)), tab((RET_HEADS, c, LANES)),
                  tab((RET_HEADS, c, LANES)), tab((RET_HEADS, 1, LANES)), tab((1, RET_WIDTH))],
        out_specs=(tok, st),
        out_shape=(jax.ShapeDtypeStruct((B, T, RET_WIDTH), F32),
                   jax.ShapeDtypeStruct((B, RET_HEADS, RET_DK, RET_DV), F32)),
        scratch_shapes=[pltpu.VMEM((bb, RET_HEADS, RET_DK, RET_DV), F32)],
        compiler_params=pltpu.CompilerParams(dimension_semantics=("parallel", "arbitrary"),
                                             vmem_limit_bytes=VMEM_LIMIT),
        name="ret",
    )(rq, rk, rv, s0, dmask, zeta, xi, gch, ret_g)


def _out_kernel(h_ref, attn_ref, ag_ref, ret_ref, rg_ref, p_ref, wo_ref, pg_ref, pp_ref, png_ref, o_ref):
    ag = ag_ref[...]
    rg = rg_ref[...]
    mix_a = (attn_ref[...] * (ag * jax.nn.sigmoid(ag))).astype(BF16)
    mix_r = (ret_ref[...] * (rg * jax.nn.sigmoid(rg))).astype(BF16)
    h1 = h_ref[...] + _dot(mix_a, wo_ref[0:ATTN_WIDTH, :]) + _dot(mix_r, wo_ref[ATTN_WIDTH:, :])
    ms = jnp.mean(h1 * h1, axis=-1, keepdims=True)
    hn = ((h1 * lax.rsqrt(ms + RMS_EPS)) * png_ref[...]).astype(BF16)
    gate = jax.nn.sigmoid(_dot(hn, pg_ref[...]))
    o_ref[...] = h1 + gate * _dot(p_ref[...].astype(BF16), pp_ref[...])


def _out_call(h, attn, ag, ret, rg, p, wo, pgate, pproj, png, tm, layer):
    B, T, _ = h.shape
    tok = lambda width: pl.BlockSpec((None, tm, width), lambda b, i: (b, i, 0))
    stacked = lambda s: pl.BlockSpec((None,) + s, lambda b, i: (layer, 0, 0))
    return pl.pallas_call(
        _out_kernel, grid=(B, T // tm),
        in_specs=[tok(D_MODEL), tok(ATTN_WIDTH), tok(ATTN_WIDTH), tok(RET_WIDTH), tok(RET_WIDTH),
                  pl.BlockSpec((None, None, tm, PLE_DIM), lambda b, i: (layer, b, i, 0)),
                  stacked((D_MODEL, D_MODEL)), stacked((D_MODEL, D_MODEL)), stacked((PLE_DIM, D_MODEL)),
                  pl.BlockSpec((1, D_MODEL), lambda b, i: (0, 0))],
        out_specs=tok(D_MODEL),
        out_shape=jax.ShapeDtypeStruct((B, T, D_MODEL), F32),
        compiler_params=pltpu.CompilerParams(dimension_semantics=("parallel", "parallel"),
                                             vmem_limit_bytes=VMEM_LIMIT),
        name="out",
    )(h, attn, ag, ret, rg, p, wo, pgate, pproj, png)


def _rope_tables(pos0, t):
    half = RET_DK // 2
    inv = ROPE_BASE ** (-jnp.arange(half, dtype=F32) / half)
    ang = (pos0 + jnp.arange(t, dtype=jnp.int32)).astype(F32)[:, None] * inv[None, :]
    cos, sin = jnp.cos(ang), jnp.sin(ang)
    return jnp.concatenate([cos, cos], axis=1), jnp.concatenate([-sin, sin], axis=1)


def _group_mean_matrix(width, group):
    i = jnp.arange(width) // group
    return jnp.where(i[:, None] == i[None, :], 1.0 / group, 0.0).astype(BF16)


def _strict_lower(n):
    return (jnp.arange(n)[None, :] < jnp.arange(n)[:, None]).astype(BF16)


def _key_rows(n_past, t):
    return -(-(n_past + t) // KEY_BLOCK) * KEY_BLOCK


def _layer(h, p, layer, depth, n_past, t_real, key_bufs, s0, lw, consts, tm, chunk, lq):
    w, wo, ng, qg, kg, ret_g, pproj, pgate, png = lw
    gq, gk, tri = consts
    B, T, _ = h.shape
    cosf, sinf = _rope_tables(n_past, T)
    buf_shape = (depth, B, _key_rows(n_past, T), LANES)
    qT, kn_buf, v_buf, ag, iqT, ikw_buf, ikwT, rq, rk, rv, rg, kidx_buf = _proj_call(
        h, ng, w, qg, kg, gq, gk, cosf, sinf, tm, key_bufs, buf_shape, layer, n_past)
    ksel = min(TOPK_MAX, (n_past + t_real) // 4)
    attn = _attn_call(iqT, qT, ikwT, ikw_buf, kn_buf, v_buf, tri, layer, n_past, t_real, ksel, lq)
    ret, s_new = _ret_call(rq, rk, rv, s0, ret_g, t_real, chunk)
    h_new = _out_call(h, attn, ag, ret, rg, p, wo, pgate, pproj, png, tm, layer)
    return h_new, (kn_buf, v_buf, ikw_buf, kidx_buf), s_new


def _permute_w_in(w_in):
    depth = w_in.shape[0]
    a = w_in[:, :, 0:1280]
    iq = w_in[:, :, 1280:1792]
    ik_iw = w_in[:, :, 1792:1864]
    r = w_in[:, :, 1864:3912]
    pad = jnp.zeros((depth, D_MODEL, W_COLS - C_IKW - 72), w_in.dtype)
    return jnp.concatenate([a, iq, r, ik_iw, pad], axis=2).astype(BF16)


def kernel(x_prompt, x_sample, cache_k, cache_v, cache_k_idx, state_ret, p_prompt, p_sample,
           w_in, w_out, norm_g, q_norm_g, k_norm_g, ret_norm_g, ple_proj, ple_gate, ple_norm_g):
    depth = w_in.shape[0]
    b_p, t_p, _ = x_prompt.shape
    b_s, t_s, _ = x_sample.shape
    past = cache_k.shape[2]
    t_s_pad = -(-t_s // LANES) * LANES

    w_perm = _permute_w_in(w_in)
    wo = w_out.astype(BF16)
    pproj = ple_proj.astype(BF16)
    pgate = ple_gate.astype(BF16)
    qg = jnp.tile(q_norm_g, (1, N_HEADS))[:, None, :]
    kg = jnp.tile(k_norm_g, (1, N_KV_HEADS))[:, None, :]
    consts = (_group_mean_matrix(ATTN_WIDTH, HEAD_DIM), _group_mean_matrix(KV_WIDTH, HEAD_DIM),
              _strict_lower(KEY_BLOCK))

    pad_t = lambda a, axis: jnp.pad(a, [(0, t_s_pad - t_s) if d == axis else (0, 0) for d in range(a.ndim)])
    h_p = x_prompt
    h_s = pad_t(x_sample, 1)
    p_s = pad_t(p_sample, 2)
    r0 = jnp.zeros((b_p, RET_HEADS, RET_DK, RET_DV), F32)
    grow = lambda a: jnp.pad(a, ((0, 0), (0, 0), (0, _key_rows(past, t_s_pad) - past), (0, LANES - a.shape[-1])))
    bufs_s = (grow(cache_k.reshape(depth, b_s, past, KV_WIDTH)), grow(cache_v.reshape(depth, b_s, past, KV_WIDTH)),
              grow(cache_k_idx), jnp.zeros((depth, b_s, t_s_pad, IDX_DIM), F32))
    bufs_p = None

    states_p, states_s = [], []
    for i in range(depth):
        lw = (w_perm, wo, norm_g[i][None], qg[i], kg[i], ret_norm_g[i][None], pproj, pgate, ple_norm_g[i][None])
        h_p, bufs_p, rp = _layer(h_p, p_prompt, i, depth, 0, t_p, bufs_p, r0, lw, consts,
                                 PROMPT_ROWS, 2 * CHUNK, 2 * LANES)
        h_s, bufs_s, rs = _layer(h_s, p_s, i, depth, past, t_s, bufs_s, state_ret[i], lw, consts,
                                 LANES, CHUNK, LANES)
        states_p.append(rp)
        states_s.append(rs)

    heads = lambda a: a.reshape(a.shape[:-1] + (N_KV_HEADS, HEAD_DIM))
    new_s = [b[:, :, past:past + t_s] for b in bufs_s[:2]]
    return (h_p, h_s[:, :t_s],
            heads(bufs_p[0]), heads(bufs_p[1]), bufs_p[3], jnp.stack(states_p),
            heads(new_s[0]), heads(new_s[1]), bufs_s[3][:, :, :t_s], jnp.stack(states_s))
```
